```python
import math
import jax
import jax.numpy as jnp
from jax import lax
import numpy as np

D_MODEL = 1024
BATCH = 16
SEQ = 256
DEPTH = 2
DEC_BATCH = 8
DEC_SEQ = 4096
PAST_LEN = 256

GRID_W = 64
HY_WIDTH = 256
HY_ORDER = 2
SHORT_CONV = 3
FILT_BANDS = 16
FILT_EMB = 2 * FILT_BANDS + 1
FILT_HID = 64
DECAY_TARGET = 1e-2
DECAY_MIN = -math.log(DECAY_TARGET) / 1.5
DECAY_MAX = -math.log(DECAY_TARGET) / 0.3
FN_WIDTH = 256
FN_GROUPS = 4
FN_GROUP_DIM = FN_WIDTH // FN_GROUPS
N_HEADS = 8
QK_NOPE_DIM = 64
ROPE_DIM = 32
V_HEAD_DIM = 64
MLA_WIDTH = N_HEADS * V_HEAD_DIM
Q_LORA_RANK = 256
KV_LORA_RANK = 128
MIX_WIDTH = HY_WIDTH + FN_WIDTH + MLA_WIDTH
IN_WIDTH = 3 * HY_WIDTH + FN_WIDTH + Q_LORA_RANK + KV_LORA_RANK + ROPE_DIM
MLA_SCALE = (QK_NOPE_DIM + ROPE_DIM) ** -0.5
AXIS_PAIRS = ROPE_DIM // 4
ROPE_BASE = 10000.0
Q_BLOCK = 128
N_EXPERTS = 32
TOP_K = 4
D_FF = 1024
SWIGLU_LIMIT = 7.0
SWIGLU_ALPHA = 1.702
MOE_BLOCK = 128
EPS = 1e-6

kernel_name = 'hybrid_diffusion_hyena_fnet_mla_moe_step'

PARAM_NAMES = ('w_mod', 'b_mod', 'g_pre_mix', 'g_post_mix', 'g_pre_ffn', 'g_post_ffn', 'w_in', 'conv_w', 'conv_b',
               'filt_w1', 'filt_b1', 'filt_w2', 'filt_b2', 'filt_w3', 'filt_freq', 'hy_log_decay', 'hy_skip',
               'q_norm_g', 'w_uq', 'kv_norm_g', 'w_ukv', 'w_out', 'router_w', 'router_b',
               'moe_w_gu', 'moe_b_gu', 'moe_w_down', 'moe_b_down')


def rmsnorm(x, g):
    x32 = x.astype(jnp.float32)
    y = x32 * lax.rsqrt(jnp.mean(x32 * x32, axis=-1, keepdims=True) + EPS)
    return (y * g.astype(jnp.float32)).astype(x.dtype)


def axial_rope_tables(n_tokens):
    n_rows = n_tokens // GRID_W
    rows = jnp.repeat(jnp.arange(n_rows, dtype=jnp.float32), GRID_W)
    cols = jnp.tile(jnp.arange(GRID_W, dtype=jnp.float32), n_rows)
    inv = ROPE_BASE ** (-jnp.arange(AXIS_PAIRS, dtype=jnp.float32) / AXIS_PAIRS)
    ang = jnp.stack([rows[:, None] * inv, cols[:, None] * inv], axis=1)
    return jnp.cos(ang), jnp.sin(ang)


def apply_axial_rope(x, cos, sin):
    xs = x.reshape(x.shape[:-1] + (2, 2, AXIS_PAIRS))
    x1, x2 = xs[..., 0, :], xs[..., 1, :]
    cos = cos.astype(x.dtype)
    sin = sin.astype(x.dtype)
    out = jnp.stack([x1 * cos - x2 * sin, x2 * cos + x1 * sin], axis=-2)
    return out.reshape(x.shape)


def hyena_filter_spectrum(n, w1, b1, w2, b2, w3, freq, log_decay):
    f32 = jnp.float32
    pos = jnp.arange(n, dtype=f32)
    t = pos / n
    bands = jnp.linspace(1e-4, FILT_BANDS - 1, FILT_BANDS, dtype=f32)
    ang = (2.0 * math.pi / n) * pos[:, None] * bands[None, :]
    feats = jnp.concatenate([t[:, None], jnp.cos(ang), -jnp.sin(ang)], axis=-1)
    freq = freq.astype(f32)
    hid = jnp.sin(freq[0] * (feats @ w1.astype(f32) + b1.astype(f32)))
    hid = jnp.sin(freq[1] * (hid @ w2.astype(f32) + b2.astype(f32)))
    h = (hid @ w3.astype(f32)).reshape(n, HY_ORDER, 2, HY_WIDTH)
    h = h * jnp.exp(-jnp.exp(log_decay.astype(f32))[None] * t[:, None, None, None])
    k = jnp.concatenate([h[:, :, 0], jnp.zeros((1, HY_ORDER, HY_WIDTH), f32), h[:0:-1, :, 1]], axis=0)
    k = k * lax.rsqrt(jnp.sum(k * k, axis=0, keepdims=True) + EPS)
    return jnp.fft.rfft(k, axis=0)


def hyena_mixer(u, p):
    dtype = u.dtype
    b, n, _ = u.shape
    pad = jnp.pad(u, ((0, 0), (1, 1), (0, 0)))
    cw = p['conv_w']
    u = pad[:, :-2] * cw[0] + pad[:, 1:-1] * cw[1] + pad[:, 2:] * cw[2] + p['conv_b']
    x1, x2, v = jnp.split(u.astype(jnp.float32), 3, axis=-1)
    spec = hyena_filter_spectrum(n, p['filt_w1'], p['filt_b1'], p['filt_w2'], p['filt_b2'], p['filt_w3'],
                                 p['filt_freq'], p['hy_log_decay'])
    skip = p['hy_skip'].astype(jnp.float32)
    z = v
    for o, gate in enumerate((x1, x2)):
        y = jnp.fft.irfft(jnp.fft.rfft(z, n=2 * n, axis=1) * spec[None, :, o], n=2 * n, axis=1)[:, :n]
        z = gate * (y + skip[o] * z)
    return z.astype(dtype)


def fourier_mixer(u):
    b, n, _ = u.shape
    f = u.astype(jnp.float32).reshape(b, n, FN_GROUPS, FN_GROUP_DIM)
    out = jnp.fft.fft2(f, axes=(1, 3), norm='ortho').real
    return out.reshape(b, n, FN_WIDTH).astype(u.dtype)


def mla_attention(q_nope, q_rope, k_nope, k_rope, v):
    b, n_q = q_nope.shape[:2]
    nb = n_q // Q_BLOCK

    def blocks(a):
        return a.reshape((b, nb, Q_BLOCK) + a.shape[2:]).swapaxes(0, 1)

    def attend(qs):
        qn, qr = qs
        s = jnp.einsum('bqhd,bkhd->bhqk', qn, k_nope) + jnp.einsum('bqhd,bkd->bhqk', qr, k_rope)
        w = jax.nn.softmax(s.astype(jnp.float32) * MLA_SCALE, axis=-1).astype(v.dtype)
        return jnp.einsum('bhqk,bkhd->bqhd', w, v)

    o = lax.map(attend, (blocks(q_nope), blocks(q_rope)))
    return o.swapaxes(0, 1).reshape(b, n_q, N_HEADS * V_HEAD_DIM)


def token_mixer(h, p, rope, ctx_ckv, ctx_krope):
    b, n, _ = h.shape
    proj = h @ p['w_in']
    o1 = 3 * HY_WIDTH
    o2 = o1 + FN_WIDTH
    o3 = o2 + Q_LORA_RANK
    hy_out = hyena_mixer(proj[..., :o1], p)
    fn_out = fourier_mixer(proj[..., o1:o2])
    q = (rmsnorm(proj[..., o2:o3], p['q_norm_g']) @ p['w_uq']).reshape(b, n, N_HEADS, QK_NOPE_DIM + ROPE_DIM)
    q_nope, q_rope = q[..., :QK_NOPE_DIM], q[..., QK_NOPE_DIM:]
    c_kv = rmsnorm(proj[..., o3:o3 + KV_LORA_RANK], p['kv_norm_g'])
    k_rope = proj[..., o3 + KV_LORA_RANK:]
    k_rope_att = k_rope
    if rope is not None:
        cos, sin = rope
        q_rope = apply_axial_rope(q_rope, cos[:, None], sin[:, None])
        k_rope_att = apply_axial_rope(k_rope, cos, sin)
    ckv_all, kr_all = c_kv, k_rope_att
    if ctx_ckv is not None:
        ckv_all = jnp.concatenate([ctx_ckv.astype(c_kv.dtype), c_kv], axis=1)
        kr_all = jnp.concatenate([ctx_krope.astype(k_rope.dtype), k_rope_att], axis=1)
    kv = (ckv_all @ p['w_ukv']).reshape(b, ckv_all.shape[1], N_HEADS, QK_NOPE_DIM + V_HEAD_DIM)
    att = mla_attention(q_nope, q_rope, kv[..., :QK_NOPE_DIM], kr_all, kv[..., QK_NOPE_DIM:])
    out = jnp.concatenate([hy_out, fn_out, att], axis=-1) @ p['w_out']
    return out, c_kv, k_rope


def moe_ffn(x, p):
    n, d = x.shape
    logits = (x @ p['router_w'] + p['router_b']).astype(jnp.float32)
    top_vals, top_idx = lax.top_k(logits, TOP_K)
    gates = jax.nn.softmax(top_vals, axis=-1)
    n_pairs = n * TOP_K
    e_flat = top_idx.reshape(-1).astype(jnp.int32)
    tok_flat = jnp.arange(n_pairs, dtype=jnp.int32) // TOP_K
    order = jnp.argsort(e_flat)
    e_s, tok_s, g_s = e_flat[order], tok_flat[order], gates.reshape(-1)[order]
    counts = jax.ops.segment_sum(jnp.ones((n_pairs,), jnp.int32), e_flat, num_segments=N_EXPERTS)
    padded = (counts + MOE_BLOCK - 1) // MOE_BLOCK * MOE_BLOCK
    pad_end = jnp.cumsum(padded)
    pad_start = pad_end - padded
    start = jnp.cumsum(counts) - counts
    dest = pad_start[e_s] + jnp.arange(n_pairs, dtype=jnp.int32) - start[e_s]
    n_slots = (-(-n_pairs // MOE_BLOCK) + N_EXPERTS) * MOE_BLOCK
    n_blocks = n_slots // MOE_BLOCK
    slot_tok = jnp.full((n_slots,), n, jnp.int32).at[dest].set(tok_s)
    slot_gate = jnp.zeros((n_slots,), jnp.float32).at[dest].set(g_s)
    block_exp = jnp.minimum(jnp.searchsorted(pad_end, jnp.arange(n_blocks, dtype=jnp.int32) * MOE_BLOCK, side='right'),
                            N_EXPERTS - 1)
    x_pad = jnp.concatenate([x, jnp.zeros((1, d), x.dtype)], axis=0)
    xb = x_pad[slot_tok].reshape(n_blocks, MOE_BLOCK, d)
    w_gu, b_gu, w_dn, b_dn = p['moe_w_gu'], p['moe_b_gu'], p['moe_w_down'], p['moe_b_down']

    def expert(args):
        xi, e = args
        gu = xi @ w_gu[e] + b_gu[e]
        g = jnp.minimum(gu[..., ::2], SWIGLU_LIMIT)
        lin = jnp.clip(gu[..., 1::2], -SWIGLU_LIMIT, SWIGLU_LIMIT)
        act = g * jax.nn.sigmoid(SWIGLU_ALPHA * g) * (lin + 1.0)
        return act @ w_dn[e] + b_dn[e]

    out = lax.map(expert, (xb, block_exp)).reshape(n_slots, d)
    y = jnp.zeros((n + 1, d), x.dtype).at[slot_tok].add(out * slot_gate[:, None].astype(x.dtype))
    return y[:n]


def trunk_layer(x, mod, p, rope, ctx_ckv, ctx_krope):
    sh1, sc1, g1, sh2, sc2, g2 = jnp.split(mod, 6, axis=-1)
    h = rmsnorm(x, p['g_pre_mix']) * (1.0 + sc1) + sh1
    m, c_kv, k_rope = token_mixer(h, p, rope, ctx_ckv, ctx_krope)
    x = x + g1 * rmsnorm(m, p['g_post_mix'])
    h = rmsnorm(x, p['g_pre_ffn']) * (1.0 + sc2) + sh2
    b, n, d = x.shape
    f = moe_ffn(h.reshape(b * n, d), p).reshape(b, n, d)
    x = x + g2 * rmsnorm(f, p['g_post_ffn'])
    return x, c_kv, k_rope


def layer_params(stacked, l):
    return {name: arr[l] for name, arr in stacked.items()}


def setup_inputs(seed: int = 0) -> dict:
    key = jax.random.key(seed)
    keys = jax.random.split(key, 64)
    counter = [0]

    def nrm(shape, scale):
        k = keys[counter[0]]
        counter[0] += 1
        return jax.random.normal(k, shape, jnp.float32) * scale

    def gain(shape):
        return 1.0 + nrm(shape, 0.02)

    decay_base = jnp.log(jnp.linspace(DECAY_MIN, DECAY_MAX, HY_WIDTH, dtype=jnp.float32))
    return {
        'x_prompt': nrm((BATCH, SEQ, D_MODEL), 1.0),
        'x_sample': nrm((DEC_BATCH, DEC_SEQ, D_MODEL), 1.0),
        'cache_ckv': nrm((DEC_BATCH, DEPTH, PAST_LEN, KV_LORA_RANK), 1.0),
        'cache_krope': nrm((DEC_BATCH, DEPTH, PAST_LEN, ROPE_DIM), 1.0),
        'c': nrm((DEC_BATCH, D_MODEL), 1.0),
        'c_ctx': nrm((D_MODEL,), 1.0),
        'w_mod': nrm((DEPTH, D_MODEL, 6 * D_MODEL), 0.5 * D_MODEL ** -0.5),
        'b_mod': nrm((DEPTH, 6 * D_MODEL), 0.02),
        'g_pre_mix': gain((DEPTH, D_MODEL)),
        'g_post_mix': gain((DEPTH, D_MODEL)),
        'g_pre_ffn': gain((DEPTH, D_MODEL)),
        'g_post_ffn': gain((DEPTH, D_MODEL)),
        'w_in': nrm((DEPTH, D_MODEL, IN_WIDTH), D_MODEL ** -0.5),
        'conv_w': nrm((DEPTH, SHORT_CONV, 3 * HY_WIDTH), SHORT_CONV ** -0.5),
        'conv_b': nrm((DEPTH, 3 * HY_WIDTH), 0.02),
        'filt_w1': nrm((DEPTH, FILT_EMB, FILT_HID), FILT_EMB ** -0.5),
        'filt_b1': nrm((DEPTH, FILT_HID), 0.1),
        'filt_w2': nrm((DEPTH, FILT_HID, FILT_HID), FILT_HID ** -0.5),
        'filt_b2': nrm((DEPTH, FILT_HID), 0.1),
        'filt_w3': nrm((DEPTH, FILT_HID, HY_ORDER * 2 * HY_WIDTH), FILT_HID ** -0.5),
        'filt_freq': 1.0 + nrm((DEPTH, 2, FILT_HID), 0.1),
        'hy_log_decay': decay_base + nrm((DEPTH, HY_ORDER, 2, HY_WIDTH), 0.05),
        'hy_skip': nrm((DEPTH, HY_ORDER, HY_WIDTH), 0.5),
        'q_norm_g': gain((DEPTH, Q_LORA_RANK)),
        'w_uq': nrm((DEPTH, Q_LORA_RANK, N_HEADS * (QK_NOPE_DIM + ROPE_DIM)), Q_LORA_RANK ** -0.5),
        'kv_norm_g': gain((DEPTH, KV_LORA_RANK)),
        'w_ukv': nrm((DEPTH, KV_LORA_RANK, N_HEADS * (QK_NOPE_DIM + V_HEAD_DIM)), KV_LORA_RANK ** -0.5),
        'w_out': nrm((DEPTH, MIX_WIDTH, D_MODEL), MIX_WIDTH ** -0.5),
        'router_w': nrm((DEPTH, D_MODEL, N_EXPERTS), D_MODEL ** -0.5),
        'router_b': nrm((DEPTH, N_EXPERTS), 0.01),
        'moe_w_gu': nrm((DEPTH, N_EXPERTS, D_MODEL, 2 * D_FF), D_MODEL ** -0.5),
        'moe_b_gu': nrm((DEPTH, N_EXPERTS, 2 * D_FF), 0.02),
        'moe_w_down': nrm((DEPTH, N_EXPERTS, D_FF, D_MODEL), D_FF ** -0.5),
        'moe_b_down': nrm((DEPTH, N_EXPERTS, D_MODEL), 0.02),
    }


def reference(x_prompt, x_sample, cache_ckv, cache_krope, c, c_ctx, w_mod, b_mod, g_pre_mix, g_post_mix,
              g_pre_ffn, g_post_ffn, w_in, conv_w, conv_b, filt_w1, filt_b1, filt_w2, filt_b2, filt_w3,
              filt_freq, hy_log_decay, hy_skip, q_norm_g, w_uq, kv_norm_g, w_ukv, w_out, router_w, router_b,
              moe_w_gu, moe_b_gu, moe_w_down, moe_b_down):
    stacked = {'w_mod': w_mod, 'b_mod': b_mod, 'g_pre_mix': g_pre_mix, 'g_post_mix': g_post_mix,
               'g_pre_ffn': g_pre_ffn, 'g_post_ffn': g_post_ffn, 'w_in': w_in, 'conv_w': conv_w,
               'conv_b': conv_b, 'filt_w1': filt_w1, 'filt_b1': filt_b1, 'filt_w2': filt_w2,
               'filt_b2': filt_b2, 'filt_w3': filt_w3, 'filt_freq': filt_freq, 'hy_log_decay': hy_log_decay,
               'hy_skip': hy_skip, 'q_norm_g': q_norm_g, 'w_uq': w_uq, 'kv_norm_g': kv_norm_g,
               'w_ukv': w_ukv, 'w_out': w_out, 'router_w': router_w, 'router_b': router_b,
               'moe_w_gu': moe_w_gu, 'moe_b_gu': moe_b_gu, 'moe_w_down': moe_w_down, 'moe_b_down': moe_b_down}

    y_prompt = x_prompt
    ctx_ckv, ctx_krope = [], []
    for l in range(DEPTH):
        p = layer_params(stacked, l)
        mod = (jax.nn.silu(c_ctx) @ p['w_mod'] + p['b_mod'])[None, None, :]
        y_prompt, ckv_l, kr_l = trunk_layer(y_prompt, mod, p, None, None, None)
        ctx_ckv.append(ckv_l)
        ctx_krope.append(kr_l)

    rope = axial_rope_tables(x_sample.shape[1])
    y_sample = x_sample
    for l in range(DEPTH):
        p = layer_params(stacked, l)
        mod = (jax.nn.silu(c) @ p['w_mod'] + p['b_mod'])[:, None, :]
        y_sample, _, _ = trunk_layer(y_sample, mod, p, rope, cache_ckv[:, l], cache_krope[:, l])

    new_cache_ckv = jnp.stack(ctx_ckv, axis=1)
    new_cache_krope = jnp.stack(ctx_krope, axis=1)
    return (y_prompt, y_sample, new_cache_ckv, new_cache_krope)
```

```python
import functools
import math

import numpy as np
import jax
import jax.numpy as jnp
from jax import lax
from jax.experimental import pallas as pl
from jax.experimental.pallas import tpu as pltpu

F32 = jnp.float32
BF16 = jnp.bfloat16

D_MODEL = 1024
DEPTH = 2
GRID_W = 64
HY_WIDTH = 256
HY_ORDER = 2
FILT_BANDS = 16
FILT_EMB = 2 * FILT_BANDS + 1
FILT_HID = 64
FN_WIDTH = 256
FN_GROUP_DIM = 64
N_HEADS = 8
QK_NOPE_DIM = 64
ROPE_DIM = 32
V_HEAD_DIM = 64
Q_LORA_RANK = 256
KV_LORA_RANK = 128
MLA_SCALE = (QK_NOPE_DIM + ROPE_DIM) ** -0.5
AXIS_PAIRS = ROPE_DIM // 4
ROPE_BASE = 10000.0
N_EXPERTS = 32
TOP_K = 4
D_FF = 1024
SWIGLU_LIMIT = 7.0
SWIGLU_ALPHA = 1.702
EPS = 1e-6

LANES = 128
VMEM_LIMIT = 56 * 1024 * 1024
ROW_TILE = 256
ATTN_Q_TILE = 128
ROUTE_TILE = 512
MOE_BLOCK = 256
DMA_ROWS = 256
FNET_TILE = 512
HEAD_PAD = 128
FFT_N1 = 64
FFT_N2 = 128
IN_SLAB = 3 * HY_WIDTH + FN_WIDTH + Q_LORA_RANK + KV_LORA_RANK + LANES


def _cparams(*sem):
    return pltpu.CompilerParams(dimension_semantics=tuple(sem), vmem_limit_bytes=VMEM_LIMIT)


def _rms(x, g):
    return x * lax.rsqrt(jnp.mean(x * x, axis=-1, keepdims=True) + EPS) * g


def _dot(a, b):
    return jnp.dot(a, b, preferred_element_type=F32)


def _dot_hi(a, b):
    return jnp.dot(a, b, preferred_element_type=F32, precision=lax.Precision.HIGHEST)


def _mod_kernel(c_ref, w_ref, b_ref, o_ref):
    c = c_ref[...]
    s = c * jax.nn.sigmoid(c)
    o_ref[0] = _dot_hi(s, w_ref[0]) + b_ref[0]


def _modulation(cc, w_mod, b_mod):
    r = cc.shape[0]
    tn = 1536
    return pl.pallas_call(
        _mod_kernel,
        grid=(DEPTH, 6 * D_MODEL // tn),
        in_specs=[pl.BlockSpec((r, D_MODEL), lambda l, j: (0, 0)),
                  pl.BlockSpec((1, D_MODEL, tn), lambda l, j: (l, 0, j)),
                  pl.BlockSpec((1, 1, tn), lambda l, j: (l, 0, j))],
        out_specs=pl.BlockSpec((1, r, tn), lambda l, j: (l, 0, j)),
        out_shape=jax.ShapeDtypeStruct((DEPTH, r, 6 * D_MODEL), F32),
        compiler_params=_cparams("parallel", "parallel"),
        name="modulation",
    )(cc, w_mod, b_mod.reshape(DEPTH, 1, 6 * D_MODEL))


def _in_proj_kernel(x_ref, mod_ref, gpre_ref, win_ref, gq_ref, wuq_ref, gkv_ref, cosf_ref, sinf_ref,
                    hy_ref, fn_ref, q_ref, ckv_ref, ks_ref):
    x = x_ref[0]
    sh1 = mod_ref[0, 0:1, :]
    sc1 = mod_ref[0, 1:2, :]
    h = _rms(x, gpre_ref[...]) * (1.0 + sc1) + sh1
    proj = _dot(h.astype(BF16), win_ref[...])
    o1 = 3 * HY_WIDTH
    o2 = o1 + FN_WIDTH
    o3 = o2 + Q_LORA_RANK
    o4 = o3 + KV_LORA_RANK
    hy_ref[0] = proj[:, :o1].astype(BF16)
    fn_ref[0] = proj[:, o1:o2].astype(BF16)
    qlat = _rms(proj[:, o2:o3], gq_ref[...]).astype(BF16)
    q2 = _dot(qlat, wuq_ref[...])
    hw = N_HEADS * HEAD_PAD
    q = q2[:, :hw] * cosf_ref[...] + q2[:, hw:] * sinf_ref[...]
    q_ref[0] = q.astype(BF16)
    ckv_ref[0] = _rms(proj[:, o3:o4], gkv_ref[...])
    ks_ref[0] = proj[:, o4:]


def _in_proj(x, mod, g_pre, w_all, g_q, w_uq_aug, g_kv, cosf, sinf):
    b, n, _ = x.shape
    tm = ROW_TILE
    hw = N_HEADS * HEAD_PAD
    mod_map = (lambda bi, i: (bi, 0, 0)) if mod.shape[0] > 1 else (lambda bi, i: (0, 0, 0))
    const = lambda bi, i: (0, 0)
    return pl.pallas_call(
        _in_proj_kernel,
        grid=(b, n // tm),
        in_specs=[pl.BlockSpec((1, tm, D_MODEL), lambda bi, i: (bi, i, 0)),
                  pl.BlockSpec((1, 6, D_MODEL), mod_map),
                  pl.BlockSpec((1, D_MODEL), const),
                  pl.BlockSpec((D_MODEL, IN_SLAB), const),
                  pl.BlockSpec((1, Q_LORA_RANK), const),
                  pl.BlockSpec((Q_LORA_RANK, 2 * hw), const),
                  pl.BlockSpec((1, KV_LORA_RANK), const),
                  pl.BlockSpec((tm, hw), lambda bi, i: (i, 0)),
                  pl.BlockSpec((tm, hw), lambda bi, i: (i, 0))],
        out_specs=[pl.BlockSpec((1, tm, 3 * HY_WIDTH), lambda bi, i: (bi, i, 0)),
                   pl.BlockSpec((1, tm, FN_WIDTH), lambda bi, i: (bi, i, 0)),
                   pl.BlockSpec((1, tm, hw), lambda bi, i: (bi, i, 0)),
                   pl.BlockSpec((1, tm, KV_LORA_RANK), lambda bi, i: (bi, i, 0)),
                   pl.BlockSpec((1, tm, LANES), lambda bi, i: (bi, i, 0))],
        out_shape=[jax.ShapeDtypeStruct((b, n, 3 * HY_WIDTH), BF16),
                   jax.ShapeDtypeStruct((b, n, FN_WIDTH), BF16),
                   jax.ShapeDtypeStruct((b, n, hw), BF16),
                   jax.ShapeDtypeStruct((b, n, KV_LORA_RANK), F32),
                   jax.ShapeDtypeStruct((b, n, LANES), F32)],
        compiler_params=_cparams("parallel", "parallel"),
        name="in_proj",
    )(x, mod, g_pre, w_all, g_q, w_uq_aug, g_kv, cosf, sinf)


def _filter_taps(feats_ref, w1_ref, b1_ref, w2_ref, b2_ref, w3f_ref, w3b_ref, freq_ref, ld_ref):
    rows = feats_ref.shape[0]
    n = rows // 2
    feats = feats_ref[...]
    hid = jnp.sin(freq_ref[0:1, :] * (_dot_hi(feats, w1_ref[...]) + b1_ref[...]))
    hid = jnp.sin(freq_ref[1:2, :] * (_dot_hi(hid, w2_ref[...]) + b2_ref[...]))
    t = feats[:, 0:1]
    hf = _dot_hi(hid, w3f_ref[0]) * jnp.exp(-jnp.exp(ld_ref[0, 0:1, :]) * t)
    hb = _dot_hi(hid, w3b_ref[0]) * jnp.exp(-jnp.exp(ld_ref[0, 1:2, :]) * t)
    r = lax.broadcasted_iota(jnp.int32, hf.shape, 0)
    k = jnp.where(r < n, hf, jnp.where(r > n, hb, 0.0))
    return k * lax.rsqrt(jnp.sum(k * k, axis=0, keepdims=True) + EPS)


def _filter_dense_kernel(feats_ref, w1_ref, b1_ref, w2_ref, b2_ref, w3f_ref, w3b_ref, freq_ref, ld_ref,
                         md_ref, spec_ref):
    k = _filter_taps(feats_ref, w1_ref, b1_ref, w2_ref, b2_ref, w3f_ref, w3b_ref, freq_ref, ld_ref)
    nn = k.shape[0]
    z = _dot(md_ref[...], k.astype(BF16))
    spec_ref[0, 0] = z[:nn].astype(BF16)
    spec_ref[0, 1] = z[nn:].astype(BF16)


def _fft_outer_forward(src_ref, m1_ref, bre_ref, bim_ref, k1):
    def body(s2, carry):
        xs = src_ref[pl.ds(s2, k1, stride=FFT_N2), :].astype(BF16)
        r = _dot(m1_ref[s2], xs)
        base = pl.multiple_of(s2 * FFT_N1, FFT_N1)
        bre_ref[pl.ds(base, FFT_N1), :] = r[:FFT_N1]
        bim_ref[pl.ds(base, FFT_N1), :] = r[FFT_N1:]
        return carry
    lax.fori_loop(0, FFT_N2, body, 0)


def _fft_inner_block(bre_ref, bim_ref, mf_ref, f1):
    xr = bre_ref[pl.ds(f1, FFT_N2, stride=FFT_N1), :]
    xi = bim_ref[pl.ds(f1, FFT_N2, stride=FFT_N1), :]
    xx = jnp.concatenate([xr, xi], axis=0).astype(BF16)
    z = _dot(mf_ref[...], xx)
    return z[:FFT_N2], z[FFT_N2:]


def _filter_fft_kernel(feats_ref, w1_ref, b1_ref, w2_ref, b2_ref, w3f_ref, w3b_ref, freq_ref, ld_ref,
                       m1_ref, mf_ref, spec_ref, k_ref, bre_ref, bim_ref):
    k_ref[...] = _filter_taps(feats_ref, w1_ref, b1_ref, w2_ref, b2_ref, w3f_ref, w3b_ref, freq_ref, ld_ref)
    _fft_outer_forward(k_ref, m1_ref, bre_ref, bim_ref, FFT_N1)

    def body(f1, carry):
        zr, zi = _fft_inner_block(bre_ref, bim_ref, mf_ref, f1)
        base = pl.multiple_of(f1 * FFT_N2, FFT_N2)
        spec_ref[0, 0, pl.ds(base, FFT_N2), :] = zr.astype(BF16)
        spec_ref[0, 1, pl.ds(base, FFT_N2), :] = zi.astype(BF16)
        return carry
    lax.fori_loop(0, FFT_N1, body, 0)


def _hyena_spectrum(n, feats, w1, b1, w2, b2, w3, freq, ld, tables):
    nn = 2 * n
    nchunk = HY_WIDTH // LANES
    c2 = lambda o, c: (0, 0)
    in_specs = [pl.BlockSpec((nn, LANES), c2),
                pl.BlockSpec((LANES, LANES), c2), pl.BlockSpec((1, LANES), c2),
                pl.BlockSpec((LANES, LANES), c2), pl.BlockSpec((1, LANES), c2),
                pl.BlockSpec((1, LANES, LANES), lambda o, c: (o * 2 * nchunk + c, 0, 0)),
                pl.BlockSpec((1, LANES, LANES), lambda o, c: (o * 2 * nchunk + nchunk + c, 0, 0)),
                pl.BlockSpec((2, LANES), c2),
                pl.BlockSpec((1, 2, LANES), lambda o, c: (o * nchunk + c, 0, 0))]
    args = [feats, w1, b1, w2, b2, w3, w3, freq, ld]
    out_spec = pl.BlockSpec((1, 2, nn, LANES), lambda o, c: (o, 0, 0, c))
    out_shape = jax.ShapeDtypeStruct((HY_ORDER, 2, nn, HY_WIDTH), BF16)
    if "md" in tables:
        return pl.pallas_call(
            _filter_dense_kernel, grid=(HY_ORDER, nchunk),
            in_specs=in_specs + [pl.BlockSpec((2 * nn, nn), c2)],
            out_specs=out_spec, out_shape=out_shape,
            compiler_params=_cparams("parallel", "parallel"), name="hyena_filter_dense",
        )(*args, tables["md"])
    return pl.pallas_call(
        _filter_fft_kernel, grid=(HY_ORDER, nchunk),
        in_specs=in_specs + [pl.BlockSpec((FFT_N2, 2 * FFT_N1, FFT_N1), lambda o, c: (0, 0, 0)),
                             pl.BlockSpec((2 * FFT_N2, 2 * FFT_N2), c2)],
        out_specs=out_spec, out_shape=out_shape,
        scratch_shapes=[pltpu.VMEM((nn, LANES), F32), pltpu.VMEM((nn, LANES), F32), pltpu.VMEM((nn, LANES), F32)],
        compiler_params=_cparams("parallel", "parallel"), name="hyena_filter_fft",
    )(*args, tables["m1_full"], tables["mf"])


def _short_conv(u, cw, cb):
    n = u.shape[0]
    row = lax.broadcasted_iota(jnp.int32, u.shape, 0)
    prev = jnp.where(row == 0, 0.0, pltpu.roll(u, 1, 0))
    nxt = jnp.where(row == n - 1, 0.0, pltpu.roll(u, n - 1, 0))
    return prev * cw[0:1, :] + u * cw[1:2, :] + nxt * cw[2:3, :] + cb


def _hyena_dense_kernel(x1_ref, x2_ref, v_ref, cw1_ref, cw2_ref, cwv_ref, cb1_ref, cb2_ref, cbv_ref, skip_ref,
                        spec_ref, md_ref, mdi_ref, o_ref):
    n = v_ref.shape[1]
    nn = 2 * n
    z = _short_conv(v_ref[0].astype(F32), cwv_ref[...], cbv_ref[...])
    gates = ((x1_ref, cw1_ref, cb1_ref), (x2_ref, cw2_ref, cb2_ref))
    for o, (g_ref, cw_ref, cb_ref) in enumerate(gates):
        zf = _dot(md_ref[...], z.astype(BF16))
        zr, zi = zf[:nn], zf[nn:]
        sr = spec_ref[o, 0].astype(F32)
        si = spec_ref[o, 1].astype(F32)
        yy = jnp.concatenate([zr * sr - zi * si, zr * si + zi * sr], axis=0).astype(BF16)
        y = _dot(mdi_ref[...], yy) * (1.0 / nn)
        gate = _short_conv(g_ref[0].astype(F32), cw_ref[...], cb_ref[...])
        z = gate * (y + skip_ref[o:o + 1, :] * z)
    o_ref[0] = z.astype(BF16)


def _hyena_fft_kernel(x1_ref, x2_ref, v_ref, cw1_ref, cw2_ref, cwv_ref, cb1_ref, cb2_ref, cbv_ref, skip_ref,
                      spec_ref, m1_ref, mf_ref, mi_ref, m3_ref, o_ref,
                      z_ref, bre_ref, bim_ref, pre_ref, pim_ref):
    n = v_ref.shape[1]
    nn = 2 * n
    k1 = FFT_N1 // 2
    z_ref[...] = _short_conv(v_ref[0].astype(F32), cwv_ref[...], cbv_ref[...])
    gates = ((x1_ref, cw1_ref, cb1_ref), (x2_ref, cw2_ref, cb2_ref))
    for o, (g_ref, cw_ref, cb_ref) in enumerate(gates):
        _fft_outer_forward(z_ref, m1_ref, bre_ref, bim_ref, k1)

        def mid(f1, carry):
            zr, zi = _fft_inner_block(bre_ref, bim_ref, mf_ref, f1)
            base = pl.multiple_of(f1 * FFT_N2, FFT_N2)
            sr = spec_ref[o, 0, pl.ds(base, FFT_N2), :].astype(F32)
            si = spec_ref[o, 1, pl.ds(base, FFT_N2), :].astype(F32)
            yy = jnp.concatenate([zr * sr - zi * si, zr * si + zi * sr], axis=0).astype(BF16)
            p = _dot(mi_ref[...], yy)
            pre_ref[pl.ds(base, FFT_N2), :] = p[:FFT_N2]
            pim_ref[pl.ds(base, FFT_N2), :] = p[FFT_N2:]
            return carry
        lax.fori_loop(0, FFT_N1, mid, 0)

        def last(t2, carry):
            pr = pre_ref[pl.ds(t2, FFT_N1, stride=FFT_N2), :]
            pi_ = pim_ref[pl.ds(t2, FFT_N1, stride=FFT_N2), :]
            xx = jnp.concatenate([pr, pi_], axis=0).astype(BF16)
            yv = _dot(m3_ref[t2], xx) * (1.0 / nn)
            bre_ref[pl.ds(t2, k1, stride=FFT_N2), :] = yv
            return carry
        lax.fori_loop(0, FFT_N2, last, 0)

        gate = _short_conv(g_ref[0].astype(F32), cw_ref[...], cb_ref[...])
        z_ref[...] = gate * (bre_ref[pl.ds(0, n), :] + skip_ref[o:o + 1, :] * z_ref[...])
    o_ref[0] = z_ref[...].astype(BF16)


def _hyena_conv(u_hy, conv_w, conv_b, skip, spec, tables):
    b, n, _ = u_hy.shape
    nn = 2 * n
    nchunk = HY_WIDTH // LANES
    cb = conv_b.reshape(1, 3 * HY_WIDTH)

    def part(p):
        return pl.BlockSpec((1, n, LANES), lambda c, bi: (bi, 0, p * nchunk + c))

    def cwp(p):
        return pl.BlockSpec((3, LANES), lambda c, bi: (0, p * nchunk + c))

    def cbp(p):
        return pl.BlockSpec((1, LANES), lambda c, bi: (0, p * nchunk + c))

    in_specs = [part(0), part(1), part(2), cwp(0), cwp(1), cwp(2), cbp(0), cbp(1), cbp(2),
                pl.BlockSpec((HY_ORDER, LANES), lambda c, bi: (0, c)),
                pl.BlockSpec((HY_ORDER, 2, nn, LANES), lambda c, bi: (0, 0, 0, c), pipeline_mode=pl.Buffered(1))]
    args = [u_hy, u_hy, u_hy, conv_w, conv_w, conv_w, cb, cb, cb, skip, spec]
    out_spec = pl.BlockSpec((1, n, LANES), lambda c, bi: (bi, 0, c))
    out_shape = jax.ShapeDtypeStruct((b, n, HY_WIDTH), BF16)
    c2 = lambda c, bi: (0, 0)
    c3 = lambda c, bi: (0, 0, 0)
    if "md" in tables:
        return pl.pallas_call(
            _hyena_dense_kernel, grid=(nchunk, b),
            in_specs=in_specs + [pl.BlockSpec((2 * nn, n), c2), pl.BlockSpec((n, 2 * nn), c2)],
            out_specs=out_spec, out_shape=out_shape,
            compiler_params=_cparams("parallel", "parallel"), name="hyena_conv_dense",
        )(*args, tables["md_half"], tables["mdi"])
    one = pl.Buffered(1)
    return pl.pallas_call(
        _hyena_fft_kernel, grid=(nchunk, b),
        in_specs=in_specs + [pl.BlockSpec((FFT_N2, 2 * FFT_N1, FFT_N1 // 2), c3, pipeline_mode=one),
                             pl.BlockSpec((2 * FFT_N2, 2 * FFT_N2), c2),
                             pl.BlockSpec((2 * FFT_N2, 2 * FFT_N2), c2),
                             pl.BlockSpec((FFT_N2, FFT_N1 // 2, 2 * FFT_N1), c3, pipeline_mode=one)],
        out_specs=out_spec, out_shape=out_shape,
        scratch_shapes=[pltpu.VMEM((n, LANES), F32)] + [pltpu.VMEM((nn, LANES), F32)] * 4,
        compiler_params=_cparams("parallel", "parallel"), name="hyena_conv_fft",
    )(*args, tables["m1_half"], tables["mf"], tables["mi"], tables["m3"])


def _fnet_kernel(u_ref, bc_ref, bs_ref, lhs_ref, o_ref, xcs_ref):
    n = u_ref.shape[1]

    @pl.when(pl.program_id(1) == 0)
    def _():
        u = u_ref[0]
        xcs_ref[pl.ds(0, n), :] = _dot(u, bc_ref[...]).astype(BF16)
        xcs_ref[pl.ds(n, n), :] = _dot(u, bs_ref[...]).astype(BF16)

    scale = 1.0 / math.sqrt(n * FN_GROUP_DIM)
    o_ref[0] = (_dot(lhs_ref[...], xcs_ref[...]) * scale).astype(BF16)


def _fnet(u_fn, tables):
    b, n, _ = u_fn.shape
    tq = min(FNET_TILE, n)
    return pl.pallas_call(
        _fnet_kernel, grid=(b, n // tq),
        in_specs=[pl.BlockSpec((1, n, FN_WIDTH), lambda bi, i: (bi, 0, 0)),
                  pl.BlockSpec((FN_WIDTH, FN_WIDTH), lambda bi, i: (0, 0)),
                  pl.BlockSpec((FN_WIDTH, FN_WIDTH), lambda bi, i: (0, 0)),
                  pl.BlockSpec((tq, 2 * n), lambda bi, i: (i, 0))],
        out_specs=pl.BlockSpec((1, tq, FN_WIDTH), lambda bi, i: (bi, i, 0)),
        out_shape=jax.ShapeDtypeStruct((b, n, FN_WIDTH), BF16),
        scratch_shapes=[pltpu.VMEM((2 * n, FN_WIDTH), BF16)],
        compiler_params=_cparams("parallel", "arbitrary"), name="fnet",
    )(u_fn, tables["bc"], tables["bs"], tables["fn_lhs"])


def _kv_kernel(ckv_ref, ks_ref, cosk_ref, wk_ref, wv_ref, k_ref, v_ref):
    ckv = ckv_ref[0].astype(BF16)
    kr = (ks_ref[0] * cosk_ref[...]).astype(BF16)
    k_ref[0] = _dot(jnp.concatenate([ckv, kr], axis=-1), wk_ref[...]).astype(BF16)
    v_ref[0] = _dot(ckv, wv_ref[...]).astype(BF16)


def _kv_up(ckv_all, ks_all, cosk, wk_aug, wv):
    b, lk, _ = ckv_all.shape
    tk = ROW_TILE
    hw = N_HEADS * HEAD_PAD
    vw = N_HEADS * V_HEAD_DIM
    return pl.pallas_call(
        _kv_kernel, grid=(b, lk // tk),
        in_specs=[pl.BlockSpec((1, tk, KV_LORA_RANK), lambda bi, i: (bi, i, 0)),
                  pl.BlockSpec((1, tk, LANES), lambda bi, i: (bi, i, 0)),
                  pl.BlockSpec((tk, LANES), lambda bi, i: (i, 0)),
                  pl.BlockSpec((KV_LORA_RANK + LANES, hw), lambda bi, i: (0, 0)),
                  pl.BlockSpec((KV_LORA_RANK, vw), lambda bi, i: (0, 0))],
        out_specs=[pl.BlockSpec((1, tk, hw), lambda bi, i: (bi, i, 0)),
                   pl.BlockSpec((1, tk, vw), lambda bi, i: (bi, i, 0))],
        out_shape=[jax.ShapeDtypeStruct((b, lk, hw), BF16), jax.ShapeDtypeStruct((b, lk, vw), BF16)],
        compiler_params=_cparams("parallel", "parallel"), name="kv_up",
    )(ckv_all, ks_all, cosk, wk_aug, wv)


def _attn_kernel(q_ref, k_ref, v_ref, o_ref):
    outs = []
    for h in range(N_HEADS):
        q = q_ref[0, :, h * HEAD_PAD:(h + 1) * HEAD_PAD]
        k = k_ref[0, :, h * HEAD_PAD:(h + 1) * HEAD_PAD]
        s = lax.dot_general(q, k, (((1,), (1,)), ((), ())), preferred_element_type=F32)
        m = jnp.max(s, axis=-1, keepdims=True)
        p = jnp.exp(s - m)
        l = jnp.sum(p, axis=-1, keepdims=True)
        v = v_ref[0, :, h * V_HEAD_DIM:(h + 1) * V_HEAD_DIM]
        outs.append(_dot(p.astype(BF16), v) / l)
    o_ref[0] = jnp.concatenate(outs, axis=-1).astype(BF16)


def _attention(q, k, v):
    b, n, hw = q.shape
    lk = k.shape[1]
    vw = v.shape[2]
    tq = ATTN_Q_TILE
    return pl.pallas_call(
        _attn_kernel, grid=(b, n // tq),
        in_specs=[pl.BlockSpec((1, tq, hw), lambda bi, i: (bi, i, 0)),
                  pl.BlockSpec((1, lk, hw), lambda bi, i: (bi, 0, 0)),
                  pl.BlockSpec((1, lk, vw), lambda bi, i: (bi, 0, 0))],
        out_specs=pl.BlockSpec((1, tq, vw), lambda bi, i: (bi, i, 0)),
        out_shape=jax.ShapeDtypeStruct((b, n, vw), BF16),
        compiler_params=_cparams("parallel", "parallel"), name="attention",
    )(q, k, v)


def _out_proj_kernel(hy_ref, fn_ref, att_ref, x_ref, mod_ref, wo_ref, gpost_ref, gffn_ref, rw_ref, rb_ref,
                     x1_ref, h2_ref, lg_ref):
    o1 = HY_WIDTH
    o2 = o1 + FN_WIDTH
    m = (_dot(hy_ref[0], wo_ref[:o1, :]) + _dot(fn_ref[0], wo_ref[o1:o2, :]) + _dot(att_ref[0], wo_ref[o2:, :]))
    g1 = mod_ref[0, 2:3, :]
    sh2 = mod_ref[0, 3:4, :]
    sc2 = mod_ref[0, 4:5, :]
    x1 = x_ref[0] + g1 * _rms(m, gpost_ref[...])
    x1_ref[0] = x1
    h2 = _rms(x1, gffn_ref[...]) * (1.0 + sc2) + sh2
    h2_ref[0] = h2
    lg_ref[0] = _dot_hi(h2, rw_ref[...]) + rb_ref[...]


def _out_proj(hy, fn, att, x, mod, w_out, g_post, g_ffn, rw_pad, rb_pad):
    b, n, _ = x.shape
    tm = ROW_TILE
    mix = HY_WIDTH + FN_WIDTH + N_HEADS * V_HEAD_DIM
    mod_map = (lambda bi, i: (bi, 0, 0)) if mod.shape[0] > 1 else (lambda bi, i: (0, 0, 0))
    const = lambda bi, i: (0, 0)
    row = lambda w: pl.BlockSpec((1, tm, w), lambda bi, i: (bi, i, 0))
    return pl.pallas_call(
        _out_proj_kernel, grid=(b, n // tm),
        in_specs=[row(HY_WIDTH), row(FN_WIDTH), row(N_HEADS * V_HEAD_DIM), row(D_MODEL),
                  pl.BlockSpec((1, 6, D_MODEL), mod_map),
                  pl.BlockSpec((mix, D_MODEL), const),
                  pl.BlockSpec((1, D_MODEL), const), pl.BlockSpec((1, D_MODEL), const),
                  pl.BlockSpec((D_MODEL, LANES), const), pl.BlockSpec((1, LANES), const)],
        out_specs=[row(D_MODEL), row(D_MODEL), row(LANES)],
        out_shape=[jax.ShapeDtypeStruct((b, n, D_MODEL), F32),
                   jax.ShapeDtypeStruct((b, n, D_MODEL), F32),
                   jax.ShapeDtypeStruct((b, n, LANES), F32)],
        compiler_params=_cparams("parallel", "parallel"), name="out_proj",
    )(hy, fn, att, x, mod, w_out, g_post, g_ffn, rw_pad, rb_pad)


def _route_kernel(lg_ref, info_ref, cnt_ref, carry_ref):
    i = pl.program_id(0)

    @pl.when(i == 0)
    def _():
        carry_ref[...] = jnp.zeros_like(carry_ref)

    v = lg_ref[...]
    tm = v.shape[0]
    lane = lax.broadcasted_iota(jnp.int32, v.shape, 1)
    sels, vals, idxs = [], [], []
    for _ in range(TOP_K):
        m = jnp.max(v, axis=-1, keepdims=True)
        idx = jnp.min(jnp.where(v == m, lane, LANES), axis=-1, keepdims=True)
        sel = lane == idx
        sels.append(sel)
        vals.append(m)
        idxs.append(idx)
        v = jnp.where(sel, -jnp.inf, v)
    es = [jnp.exp(val - vals[0]) for val in vals]
    den = es[0] + es[1] + es[2] + es[3]
    onehot = jnp.where(sels[0] | sels[1] | sels[2] | sels[3], 1.0, 0.0)
    r_i = lax.broadcasted_iota(jnp.int32, (tm, tm), 0)
    c_i = lax.broadcasted_iota(jnp.int32, (tm, tm), 1)
    tri = jnp.where(c_i < r_i, 1.0, 0.0).astype(BF16)
    rank = _dot(tri, onehot.astype(BF16)) + carry_ref[0:1, :]
    info = jnp.zeros(v.shape, F32)
    for kk in range(TOP_K):
        rk = jnp.sum(jnp.where(sels[kk], rank, 0.0), axis=-1, keepdims=True)
        info = jnp.where(lane == kk, es[kk] / den, info)
        info = jnp.where(lane == TOP_K + kk, idxs[kk].astype(F32), info)
        info = jnp.where(lane == 2 * TOP_K + kk, rk, info)
    info_ref[...] = info
    carry_ref[0:1, :] = carry_ref[0:1, :] + jnp.sum(onehot, axis=0, keepdims=True)
    cnt_ref[...] = carry_ref[...]


def _route(logits):
    t = logits.shape[0]
    tm = ROUTE_TILE
    return pl.pallas_call(
        _route_kernel, grid=(t // tm,),
        in_specs=[pl.BlockSpec((tm, LANES), lambda i: (i, 0))],
        out_specs=[pl.BlockSpec((tm, LANES), lambda i: (i, 0)), pl.BlockSpec((8, LANES), lambda i: (0, 0))],
        out_shape=[jax.ShapeDtypeStruct((t, LANES), F32), jax.ShapeDtypeStruct((8, LANES), F32)],
        scratch_shapes=[pltpu.VMEM((8, LANES), F32)],
        compiler_params=_cparams("arbitrary"), name="route",
    )(logits)


def _dispatch_kernel(dest_ref, h_ref, xb_in_ref, xb_ref, idx_ref, isem, sem):
    del xb_in_ref
    i = pl.program_id(0)
    rows = h_ref.shape[0]
    cp = pltpu.make_async_copy(dest_ref.at[i], idx_ref, isem)
    cp.start()
    cp.wait()

    def row_copy(r, kk, slot):
        return pltpu.make_async_copy(h_ref.at[pl.ds(r, 1)], xb_ref.at[pl.ds(slot, 1)], sem)

    def issue(r, carry):
        for kk in range(TOP_K):
            row_copy(r, kk, idx_ref[r * TOP_K + kk]).start()
        return carry
    lax.fori_loop(0, rows, issue, 0)

    def drain(r, carry):
        for kk in range(TOP_K):
            row_copy(r, kk, 0).wait()
        return carry
    lax.fori_loop(0, rows, drain, 0)


def _dispatch(h2, dest, n_slots):
    t = h2.shape[0]
    rows = DMA_ROWS
    xb0 = jnp.zeros((n_slots, D_MODEL), F32)
    return pl.pallas_call(
        _dispatch_kernel, grid=(t // rows,),
        in_specs=[pl.BlockSpec(memory_space=pl.ANY),
                  pl.BlockSpec((rows, D_MODEL), lambda i: (i, 0)),
                  pl.BlockSpec(memory_space=pl.ANY)],
        out_specs=pl.BlockSpec(memory_space=pl.ANY),
        out_shape=jax.ShapeDtypeStruct((n_slots, D_MODEL), F32),
        scratch_shapes=[pltpu.SMEM((rows * TOP_K,), jnp.int32),
                        pltpu.SemaphoreType.DMA(()), pltpu.SemaphoreType.DMA(())],
        input_output_aliases={2: 0},
        compiler_params=_cparams("arbitrary"), name="moe_dispatch",
    )(dest.reshape(t // rows, rows * TOP_K), h2, xb0)


def _expert_kernel(be_ref, nu_ref, x_ref, wg_ref, wl_ref, bg_ref, bl_ref, wd_ref, bd_ref, o_ref):
    i = pl.program_id(0)

    @pl.when(i < nu_ref[0])
    def _():
        x = x_ref[...].astype(BF16)
        g = jnp.minimum(_dot(x, wg_ref[0]) + bg_ref[0], SWIGLU_LIMIT)
        lin = jnp.clip(_dot(x, wl_ref[0]) + bl_ref[0], -SWIGLU_LIMIT, SWIGLU_LIMIT)
        act = g * jax.nn.sigmoid(SWIGLU_ALPHA * g) * (lin + 1.0)
        o_ref[...] = _dot(act.astype(BF16), wd_ref[0]) + bd_ref[0]

    @pl.when(i >= nu_ref[0])
    def _():
        o_ref[...] = jnp.zeros_like(o_ref)


def _experts(xb, block_exp, n_used, wg, wl, bg, bl, wd, bd):
    n_slots = xb.shape[0]
    bm = MOE_BLOCK
    wmap = lambda i, be, nu: (be[i], 0, 0)
    return pl.pallas_call(
        _expert_kernel,
        grid_spec=pltpu.PrefetchScalarGridSpec(
            num_scalar_prefetch=2, grid=(n_slots // bm,),
            in_specs=[pl.BlockSpec((bm, D_MODEL), lambda i, be, nu: (i, 0)),
                      pl.BlockSpec((1, D_MODEL, D_FF), wmap), pl.BlockSpec((1, D_MODEL, D_FF), wmap),
                      pl.BlockSpec((1, 1, D_FF), wmap), pl.BlockSpec((1, 1, D_FF), wmap),
                      pl.BlockSpec((1, D_FF, D_MODEL), wmap), pl.BlockSpec((1, 1, D_MODEL), wmap)],
            out_specs=pl.BlockSpec((bm, D_MODEL), lambda i, be, nu: (i, 0))),
        out_shape=jax.ShapeDtypeStruct((n_slots, D_MODEL), F32),
        compiler_params=_cparams("arbitrary"), name="moe_experts",
    )(block_exp, n_used, xb, wg, wl, bg, bl, wd, bd)


def _combine_kernel(dest_ref, ys_ref, info_ref, x1_ref, mod_ref, gpost_ref, o_ref, idx_ref, buf_ref, isem, sem):
    i = pl.program_id(1)
    nb = pl.num_programs(1)
    rows = x1_ref.shape[1]
    cp = pltpu.make_async_copy(dest_ref.at[pl.program_id(0) * nb + i], idx_ref, isem)
    cp.start()
    cp.wait()

    def row_copy(r, kk, slot):
        return pltpu.make_async_copy(ys_ref.at[pl.ds(slot, 1)], buf_ref.at[kk, pl.ds(r, 1)], sem)

    def issue(r, carry):
        for kk in range(TOP_K):
            row_copy(r, kk, idx_ref[r * TOP_K + kk]).start()
        return carry
    lax.fori_loop(0, rows, issue, 0)

    def drain(r, carry):
        for kk in range(TOP_K):
            row_copy(r, kk, 0).wait()
        return carry
    lax.fori_loop(0, rows, drain, 0)

    info = info_ref[...]
    f = info[:, 0:1] * buf_ref[0]
    for kk in range(1, TOP_K):
        f = f + info[:, kk:kk + 1] * buf_ref[kk]
    g2 = mod_ref[0, 5:6, :]
    o_ref[0] = x1_ref[0] + g2 * _rms(f, gpost_ref[...])


def _combine(ys, dest, info, x1, mod, g_post):
    b, n, _ = x1.shape
    rows = DMA_ROWS
    nb = n // rows
    mod_map = (lambda bi, i: (bi, 0, 0)) if mod.shape[0] > 1 else (lambda bi, i: (0, 0, 0))
    return pl.pallas_call(
        _combine_kernel, grid=(b, nb),
        in_specs=[pl.BlockSpec(memory_space=pl.ANY),
                  pl.BlockSpec(memory_space=pl.ANY),
                  pl.BlockSpec((rows, LANES), lambda bi, i: (bi * nb + i, 0)),
                  pl.BlockSpec((1, rows, D_MODEL), lambda bi, i: (bi, i, 0)),
                  pl.BlockSpec((1, 6, D_MODEL), mod_map),
                  pl.BlockSpec((1, D_MODEL), lambda bi, i: (0, 0))],
        out_specs=pl.BlockSpec((1, rows, D_MODEL), lambda bi, i: (bi, i, 0)),
        out_shape=jax.ShapeDtypeStruct((b, n, D_MODEL), F32),
        scratch_shapes=[pltpu.SMEM((rows * TOP_K,), jnp.int32),
                        pltpu.VMEM((TOP_K, rows, D_MODEL), F32),
                        pltpu.SemaphoreType.DMA(()), pltpu.SemaphoreType.DMA(())],
        compiler_params=_cparams("arbitrary", "arbitrary"), name="moe_combine",
    )(dest.reshape(b * nb, rows * TOP_K), ys, info, x1, mod, g_post)


def _moe_and_residual(h2, logits, x1, mod, g_post, ew):
    b, n, _ = x1.shape
    t = b * n
    info, cnt = _route(logits.reshape(t, LANES))
    counts = cnt[0, :N_EXPERTS].astype(jnp.int32)
    bm = MOE_BLOCK
    padded = (counts + bm - 1) // bm * bm
    pad_end = jnp.cumsum(padded)
    pad_start = pad_end - padded
    e_idx = info[:, TOP_K:2 * TOP_K].astype(jnp.int32)
    rank = info[:, 2 * TOP_K:3 * TOP_K].astype(jnp.int32)
    dest = (pad_start[e_idx] + rank).reshape(-1)
    n_blocks = t * TOP_K // bm + N_EXPERTS
    n_slots = n_blocks * bm
    block_exp = jnp.minimum(
        jnp.searchsorted(pad_end, jnp.arange(n_blocks, dtype=jnp.int32) * bm, side="right"),
        N_EXPERTS - 1).astype(jnp.int32)
    n_used = (pad_end[-1:] // bm).astype(jnp.int32)
    xb = _dispatch(h2.reshape(t, D_MODEL), dest, n_slots)
    ys = _experts(xb, block_exp, n_used, *ew)
    return _combine(ys, dest, info, x1, mod, g_post)


def _cos_sin(num, den, shape_like=None):
    ang = (2.0 * math.pi / den) * num.astype(F32)
    return jnp.cos(ang), jnp.sin(ang)


def _dense_dft_tables(n):
    nn = 2 * n
    f = jnp.arange(nn, dtype=jnp.int32)[:, None]
    s = jnp.arange(nn, dtype=jnp.int32)[None, :]
    c, sn = _cos_sin((f * s) % nn, nn)
    md = jnp.concatenate([c, -sn], axis=0).astype(BF16)
    ci, si = c[:n, :], sn[:n, :]
    mdi = jnp.concatenate([ci, -si], axis=1).astype(BF16)
    return {"md": md, "md_half": md[:, :n], "mdi": mdi}


def _fft_tables(n):
    nn = 2 * n
    n1, n2 = FFT_N1, FFT_N2
    assert nn == n1 * n2
    s2 = jnp.arange(n2, dtype=jnp.int32)[:, None, None]
    f1 = jnp.arange(n1, dtype=jnp.int32)[None, :, None]
    s1 = jnp.arange(n1, dtype=jnp.int32)[None, None, :]
    c, sn = _cos_sin((f1 * (n2 * s1 + s2)) % nn, nn)
    m1 = jnp.concatenate([c, -sn], axis=1).astype(BF16)
    a = jnp.arange(n2, dtype=jnp.int32)
    c2, sn2 = _cos_sin((a[:, None] * a[None, :]) % n2, n2)
    mf = jnp.concatenate([jnp.concatenate([c2, sn2], axis=1),
                          jnp.concatenate([-sn2, c2], axis=1)], axis=0).astype(BF16)
    mi = jnp.concatenate([jnp.concatenate([c2, -sn2], axis=1),
                          jnp.concatenate([sn2, c2], axis=1)], axis=0).astype(BF16)
    t2 = jnp.arange(n2, dtype=jnp.int32)[:, None, None]
    t1 = jnp.arange(n1 // 2, dtype=jnp.int32)[None, :, None]
    g1 = jnp.arange(n1, dtype=jnp.int32)[None, None, :]
    c3, sn3 = _cos_sin((g1 * (n2 * t1 + t2)) % nn, nn)
    m3 = jnp.concatenate([c3, -sn3], axis=2).astype(BF16)
    return {"m1_full": m1, "m1_half": m1[:, :, :n1 // 2], "mf": mf, "mi": mi, "m3": m3}


def _fnet_tables(n):
    j = jnp.arange(FN_WIDTH, dtype=jnp.int32)
    same = (j[:, None] // FN_GROUP_DIM) == (j[None, :] // FN_GROUP_DIM)
    c, s = _cos_sin(((j[:, None] % FN_GROUP_DIM) * (j[None, :] % FN_GROUP_DIM)) % FN_GROUP_DIM, FN_GROUP_DIM)
    bc = jnp.where(same, c, 0.0).astype(BF16)
    bs = jnp.where(same, s, 0.0).astype(BF16)
    p = jnp.arange(n, dtype=jnp.int32)
    cl, sl = _cos_sin((p[:, None] * p[None, :]) % n, n)
    return {"bc": bc, "bs": bs, "fn_lhs": jnp.concatenate([cl, -sl], axis=1).astype(BF16)}


def _filter_feats(n):
    r = jnp.arange(2 * n, dtype=jnp.int32)
    pos = jnp.where(r < n, r, 2 * n - r).astype(F32)
    t = pos / n
    bands = jnp.linspace(1e-4, FILT_BANDS - 1, FILT_BANDS, dtype=F32)
    ang = (2.0 * math.pi / n) * pos[:, None] * bands[None, :]
    feats = jnp.concatenate([t[:, None], jnp.cos(ang), -jnp.sin(ang)], axis=-1)
    return jnp.pad(feats, ((0, 0), (0, LANES - FILT_EMB)))


def _rope_tables(n, use_rope):
    if use_rope:
        tpos = jnp.arange(n, dtype=jnp.int32)
        rows = (tpos // GRID_W).astype(F32)
        cols = (tpos % GRID_W).astype(F32)
        inv = ROPE_BASE ** (-jnp.arange(AXIS_PAIRS, dtype=F32) / AXIS_PAIRS)
        ang = jnp.stack([rows[:, None] * inv, cols[:, None] * inv], axis=1)
        cos = jnp.broadcast_to(jnp.cos(ang)[:, :, None, :], (n, 2, 2, AXIS_PAIRS)).reshape(n, ROPE_DIM)
        sin = jnp.broadcast_to(jnp.sin(ang)[:, :, None, :], (n, 2, 2, AXIS_PAIRS)).reshape(n, ROPE_DIM)
    else:
        cos = jnp.ones((n, ROPE_DIM), F32)
        sin = jnp.zeros((n, ROPE_DIM), F32)
    pad = HEAD_PAD - QK_NOPE_DIM - ROPE_DIM
    cos_h = jnp.concatenate([jnp.ones((n, QK_NOPE_DIM), F32), cos, jnp.zeros((n, pad), F32)], axis=1)
    sin_h = jnp.concatenate([jnp.zeros((n, QK_NOPE_DIM), F32), sin, jnp.zeros((n, pad), F32)], axis=1)
    cosf = jnp.tile(cos_h, (1, N_HEADS)) * MLA_SCALE
    sinf = jnp.tile(sin_h, (1, N_HEADS)) * MLA_SCALE
    cosk = jnp.concatenate([cos, sin, jnp.zeros((n, LANES - 2 * ROPE_DIM), F32)], axis=1)
    return cosf, sinf, cosk


def _rot_matrix():
    r = np.zeros((ROPE_DIM, ROPE_DIM), np.float32)
    for a in range(2):
        for p in range(AXIS_PAIRS):
            lo = a * 2 * AXIS_PAIRS + p
            hi = lo + AXIS_PAIRS
            r[hi, lo] = -1.0
            r[lo, hi] = 1.0
    return jnp.asarray(r)


def _prep_layer(l, w):
    rot = _rot_matrix()
    w_in = w["w_in"][l]
    o4 = 3 * HY_WIDTH + FN_WIDTH + Q_LORA_RANK + KV_LORA_RANK
    w_kr = w_in[:, o4:]
    w_all = jnp.concatenate(
        [w_in[:, :o4], w_kr, w_kr @ rot, jnp.zeros((D_MODEL, LANES - 2 * ROPE_DIM), F32)], axis=1).astype(BF16)
    hd = QK_NOPE_DIM + ROPE_DIM
    pad = HEAD_PAD - hd
    wq = w["w_uq"][l].reshape(Q_LORA_RANK, N_HEADS, hd)
    zq = jnp.zeros((Q_LORA_RANK, N_HEADS, pad), F32)
    plain = jnp.concatenate([wq, zq], axis=2)
    rotated = jnp.concatenate([jnp.zeros((Q_LORA_RANK, N_HEADS, QK_NOPE_DIM), F32),
                               jnp.einsum("rhd,de->rhe", wq[:, :, QK_NOPE_DIM:], rot), zq], axis=2)
    hw = N_HEADS * HEAD_PAD
    w_uq_aug = jnp.concatenate([plain.reshape(Q_LORA_RANK, hw), rotated.reshape(Q_LORA_RANK, hw)], axis=1).astype(BF16)
    wkv = w["w_ukv"][l].reshape(KV_LORA_RANK, N_HEADS, QK_NOPE_DIM + V_HEAD_DIM)
    wk = jnp.concatenate([wkv[:, :, :QK_NOPE_DIM],
                          jnp.zeros((KV_LORA_RANK, N_HEADS, HEAD_PAD - QK_NOPE_DIM), F32)], axis=2).reshape(KV_LORA_RANK, hw)
    sel = np.zeros((LANES, N_HEADS, HEAD_PAD), np.float32)
    for j in range(ROPE_DIM):
        sel[j, :, QK_NOPE_DIM + j] = 1.0
        sel[ROPE_DIM + j, :, QK_NOPE_DIM + j] = 1.0
    wk_aug = jnp.concatenate([wk, jnp.asarray(sel).reshape(LANES, hw)], axis=0).astype(BF16)
    wv = wkv[:, :, QK_NOPE_DIM:].reshape(KV_LORA_RANK, N_HEADS * V_HEAD_DIM).astype(BF16)
    padh = LANES - FILT_HID
    w1 = jnp.pad(w["filt_w1"][l], ((0, LANES - FILT_EMB), (0, padh)))
    b1 = jnp.pad(w["filt_b1"][l], (0, padh)).reshape(1, LANES)
    w2 = jnp.pad(w["filt_w2"][l], ((0, padh), (0, padh)))
    b2 = jnp.pad(w["filt_b2"][l], (0, padh)).reshape(1, LANES)
    nblk = HY_ORDER * 2 * HY_WIDTH // LANES
    w3 = jnp.pad(w["filt_w3"][l], ((0, padh), (0, 0))).reshape(LANES, nblk, LANES).transpose(1, 0, 2)
    freq = jnp.pad(w["filt_freq"][l], ((0, 0), (0, padh)))
    nch = HY_WIDTH // LANES
    ld = w["hy_log_decay"][l].reshape(HY_ORDER, 2, nch, LANES).transpose(0, 2, 1, 3).reshape(HY_ORDER * nch, 2, LANES)
    rw = jnp.pad(w["router_w"][l], ((0, 0), (0, LANES - N_EXPERTS)))
    rb = jnp.pad(w["router_b"][l], (0, LANES - N_EXPERTS), constant_values=-1e30).reshape(1, LANES)
    gu = w["moe_w_gu"][l]
    bgu = w["moe_b_gu"][l]
    ew = (gu[:, :, 0::2].astype(BF16), gu[:, :, 1::2].astype(BF16),
          bgu[:, None, 0::2], bgu[:, None, 1::2],
          w["moe_w_down"][l].astype(BF16), w["moe_b_down"][l][:, None, :])
    return {
        "w_all": w_all, "w_uq_aug": w_uq_aug, "wk_aug": wk_aug, "wv": wv,
        "g_pre": w["g_pre_mix"][l].reshape(1, D_MODEL), "g_post": w["g_post_mix"][l].reshape(1, D_MODEL),
        "g_ffn": w["g_pre_ffn"][l].reshape(1, D_MODEL), "g_post_ffn": w["g_post_ffn"][l].reshape(1, D_MODEL),
        "g_q": w["q_norm_g"][l].reshape(1, Q_LORA_RANK), "g_kv": w["kv_norm_g"][l].reshape(1, KV_LORA_RANK),
        "conv_w": w["conv_w"][l], "conv_b": w["conv_b"][l], "skip": w["hy_skip"][l],
        "filt": (w1, b1, w2, b2, w3, freq, ld),
        "w_out": w["w_out"][l].astype(BF16), "rw": rw, "rb": rb, "ew": ew,
    }


def _trunk_layer(x, mod, p, pos, ctx):
    b, n, _ = x.shape
    u_hy, u_fn, q, ckv, ks = _in_proj(x, mod, p["g_pre"], p["w_all"], p["g_q"], p["w_uq_aug"], p["g_kv"],
                                      pos["cosf"], pos["sinf"])
    spec = _hyena_spectrum(n, pos["feats"], *p["filt"], pos["hy"])
    hy = _hyena_conv(u_hy, p["conv_w"], p["conv_b"], p["skip"], spec, pos["hy"])
    fn = _fnet(u_fn, pos["fn"])
    if ctx is None:
        ckv_all, ks_all, cosk = ckv, ks, pos["cosk"]
    else:
        c_ckv, c_kr = ctx
        past = c_ckv.shape[1]
        ckv_all = jnp.concatenate([c_ckv, ckv], axis=1)
        ks_all = jnp.concatenate([jnp.pad(c_kr, ((0, 0), (0, 0), (0, LANES - ROPE_DIM))), ks], axis=1)
        ident = jnp.concatenate([jnp.ones((past, ROPE_DIM), F32), jnp.zeros((past, LANES - ROPE_DIM), F32)], axis=1)
        cosk = jnp.concatenate([ident, pos["cosk"]], axis=0)
    k, v = _kv_up(ckv_all, ks_all, cosk, p["wk_aug"], p["wv"])
    att = _attention(q, k, v)
    x1, h2, logits = _out_proj(hy, fn, att, x, mod, p["w_out"], p["g_post"], p["g_ffn"], p["rw"], p["rb"])
    x2 = _moe_and_residual(h2, logits, x1, mod, p["g_post_ffn"], p["ew"])
    return x2, ckv, ks[..., :ROPE_DIM]


def _position_tables(n, use_rope, dense):
    cosf, sinf, cosk = _rope_tables(n, use_rope)
    return {"cosf": cosf, "sinf": sinf, "cosk": cosk, "feats": _filter_feats(n),
            "hy": _dense_dft_tables(n) if dense else _fft_tables(n), "fn": _fnet_tables(n)}


def kernel(x_prompt, x_sample, cache_ckv, cache_krope, c, c_ctx, w_mod, b_mod, g_pre_mix, g_post_mix, g_pre_ffn, g_post_ffn, w_in, conv_w, conv_b, filt_w1, filt_b1, filt_w2, filt_b2, filt_w3, filt_freq, hy_log_decay, hy_skip, q_norm_g, w_uq, kv_norm_g, w_ukv, w_out, router_w, router_b, moe_w_gu, moe_b_gu, moe_w_down, moe_b_down):
    w = {"g_pre_mix": g_pre_mix, "g_post_mix": g_post_mix, "g_pre_ffn": g_pre_ffn, "g_post_ffn": g_post_ffn,
         "w_in": w_in, "conv_w": conv_w, "conv_b": conv_b, "filt_w1": filt_w1, "filt_b1": filt_b1,
         "filt_w2": filt_w2, "filt_b2": filt_b2, "filt_w3": filt_w3, "filt_freq": filt_freq,
         "hy_log_decay": hy_log_decay, "hy_skip": hy_skip, "q_norm_g": q_norm_g, "w_uq": w_uq,
         "kv_norm_g": kv_norm_g, "w_ukv": w_ukv, "w_out": w_out, "router_w": router_w, "router_b": router_b,
         "moe_w_gu": moe_w_gu, "moe_b_gu": moe_b_gu, "moe_w_down": moe_w_down, "moe_b_down": moe_b_down}
    nb = c.shape[0]
    rows = 16
    cc = jnp.concatenate([c_ctx[None, :], c, jnp.zeros((rows - 1 - nb, D_MODEL), F32)], axis=0)
    mod_all = _modulation(cc, w_mod, b_mod)
    layers = [_prep_layer(l, w) for l in range(DEPTH)]
    pos_p = _position_tables(x_prompt.shape[1], False, True)
    pos_s = _position_tables(x_sample.shape[1], True, False)

    y_prompt = x_prompt
    ckvs, krs = [], []
    for l in range(DEPTH):
        mod = mod_all[l, 0:1].reshape(1, 6, D_MODEL)
        y_prompt, ckv_l, kr_l = _trunk_layer(y_prompt, mod, layers[l], pos_p, None)
        ckvs.append(ckv_l)
        krs.append(kr_l)

    y_sample = x_sample
    for l in range(DEPTH):
        mod = mod_all[l, 1:1 + nb].reshape(nb, 6, D_MODEL)
        y_sample, _, _ = _trunk_layer(y_sample, mod, layers[l], pos_s, (cache_ckv[:, l], cache_krope[:, l]))

    return (y_prompt, y_sample, jnp.stack(ckvs, axis=1), jnp.stack(krs, axis=1))
```

```python
import functools
import math

import numpy as np
import jax
import jax.numpy as jnp
from jax import lax
from jax.experimental import pallas as pl
from jax.experimental.pallas import tpu as pltpu

F32 = jnp.float32
BF16 = jnp.bfloat16

D_MODEL = 1024
DEPTH = 2
GRID_W = 64
HY_WIDTH = 256
HY_ORDER = 2
FILT_BANDS = 16
FILT_EMB = 2 * FILT_BANDS + 1
FILT_HID = 64
FN_WIDTH = 256
FN_GROUP_DIM = 64
N_HEADS = 8
QK_NOPE_DIM = 64
ROPE_DIM = 32
V_HEAD_DIM = 64
Q_LORA_RANK = 256
KV_LORA_RANK = 128
MLA_SCALE = (QK_NOPE_DIM + ROPE_DIM) ** -0.5
AXIS_PAIRS = ROPE_DIM // 4
ROPE_BASE = 10000.0
N_EXPERTS = 32
TOP_K = 4
D_FF = 1024
SWIGLU_LIMIT = 7.0
SWIGLU_ALPHA = 1.702
EPS = 1e-6

LANES = 128
VMEM_LIMIT = 56 * 1024 * 1024
ROW_TILE = 256
ATTN_Q_TILE = 256
ROUTE_TILE = 512
MOE_BLOCK = 256
DMA_ROWS = 256
FNET_TILE = 512
HEAD_PAD = 128
FFT_N1 = 64
FFT_N2 = 128
FFT_UNROLL = 8
MXU_TILE = 256
IN_SLAB = 3 * HY_WIDTH + FN_WIDTH + Q_LORA_RANK + KV_LORA_RANK + LANES


def _cparams(*sem):
    return pltpu.CompilerParams(dimension_semantics=tuple(sem), vmem_limit_bytes=VMEM_LIMIT)


def _rms(x, g):
    return x * lax.rsqrt(jnp.mean(x * x, axis=-1, keepdims=True) + EPS) * g


def _dot(a, b):
    return jnp.dot(a, b, preferred_element_type=F32)


def _dot_hi(a, b):
    return jnp.dot(a, b, preferred_element_type=F32, precision=lax.Precision.HIGHEST)


def _mod_kernel(c_ref, w_ref, b_ref, o_ref):
    c = c_ref[...]
    s = c * jax.nn.sigmoid(c)
    o_ref[0] = _dot_hi(s, w_ref[0]) + b_ref[0]


def _modulation(cc, w_mod, b_mod):
    r = cc.shape[0]
    tn = 1536
    return pl.pallas_call(
        _mod_kernel,
        grid=(DEPTH, 6 * D_MODEL // tn),
        in_specs=[pl.BlockSpec((r, D_MODEL), lambda l, j: (0, 0)),
                  pl.BlockSpec((1, D_MODEL, tn), lambda l, j: (l, 0, j)),
                  pl.BlockSpec((1, 1, tn), lambda l, j: (l, 0, j))],
        out_specs=pl.BlockSpec((1, r, tn), lambda l, j: (l, 0, j)),
        out_shape=jax.ShapeDtypeStruct((DEPTH, r, 6 * D_MODEL), F32),
        compiler_params=_cparams("parallel", "parallel"),
        name="modulation",
    )(cc, w_mod, b_mod.reshape(DEPTH, 1, 6 * D_MODEL))


def _in_proj_kernel(x_ref, mod_ref, gpre_ref, win_ref, gq_ref, wuq_ref, gkv_ref, cosf_ref, sinf_ref,
                    hy_ref, fn_ref, q_ref, ckv_ref, ks_ref):
    x = x_ref[0]
    sh1 = mod_ref[0, 0:1, :]
    sc1 = mod_ref[0, 1:2, :]
    h = _rms(x, gpre_ref[...]) * (1.0 + sc1) + sh1
    proj = _dot(h.astype(BF16), win_ref[...])
    o1 = 3 * HY_WIDTH
    o2 = o1 + FN_WIDTH
    o3 = o2 + Q_LORA_RANK
    o4 = o3 + KV_LORA_RANK
    hy_ref[0] = proj[:, :o1].astype(BF16)
    fn_ref[0] = proj[:, o1:o2].astype(BF16)
    qlat = _rms(proj[:, o2:o3], gq_ref[...]).astype(BF16)
    q2 = _dot(qlat, wuq_ref[...])
    hw = N_HEADS * HEAD_PAD
    q = q2[:, :hw] * cosf_ref[...] + q2[:, hw:] * sinf_ref[...]
    q_ref[0] = q.astype(BF16)
    ckv_ref[0] = _rms(proj[:, o3:o4], gkv_ref[...])
    ks_ref[0] = proj[:, o4:]


def _in_proj(x, mod, g_pre, w_all, g_q, w_uq_aug, g_kv, cosf, sinf):
    b, n, _ = x.shape
    tm = ROW_TILE
    hw = N_HEADS * HEAD_PAD
    mod_map = (lambda bi, i: (bi, 0, 0)) if mod.shape[0] > 1 else (lambda bi, i: (0, 0, 0))
    const = lambda bi, i: (0, 0)
    return pl.pallas_call(
        _in_proj_kernel,
        grid=(b, n // tm),
        in_specs=[pl.BlockSpec((1, tm, D_MODEL), lambda bi, i: (bi, i, 0)),
                  pl.BlockSpec((1, 6, D_MODEL), mod_map),
                  pl.BlockSpec((1, D_MODEL), const),
                  pl.BlockSpec((D_MODEL, IN_SLAB), const),
                  pl.BlockSpec((1, Q_LORA_RANK), const),
                  pl.BlockSpec((Q_LORA_RANK, 2 * hw), const),
                  pl.BlockSpec((1, KV_LORA_RANK), const),
                  pl.BlockSpec((tm, hw), lambda bi, i: (i, 0)),
                  pl.BlockSpec((tm, hw), lambda bi, i: (i, 0))],
        out_specs=[pl.BlockSpec((1, tm, 3 * HY_WIDTH), lambda bi, i: (bi, i, 0)),
                   pl.BlockSpec((1, tm, FN_WIDTH), lambda bi, i: (bi, i, 0)),
                   pl.BlockSpec((1, tm, hw), lambda bi, i: (bi, i, 0)),
                   pl.BlockSpec((1, tm, KV_LORA_RANK), lambda bi, i: (bi, i, 0)),
                   pl.BlockSpec((1, tm, LANES), lambda bi, i: (bi, i, 0))],
        out_shape=[jax.ShapeDtypeStruct((b, n, 3 * HY_WIDTH), BF16),
                   jax.ShapeDtypeStruct((b, n, FN_WIDTH), BF16),
                   jax.ShapeDtypeStruct((b, n, hw), BF16),
                   jax.ShapeDtypeStruct((b, n, KV_LORA_RANK), F32),
                   jax.ShapeDtypeStruct((b, n, LANES), F32)],
        compiler_params=_cparams("parallel", "parallel"),
        name="in_proj",
    )(x, mod, g_pre, w_all, g_q, w_uq_aug, g_kv, cosf, sinf)


def _filter_taps(feats_ref, w1_ref, b1_ref, w2_ref, b2_ref, w3f_ref, w3b_ref, freq_ref, ld_ref):
    rows = feats_ref.shape[0]
    n = rows // 2
    feats = feats_ref[...]
    hid = jnp.sin(freq_ref[0:1, :] * (_dot_hi(feats, w1_ref[...]) + b1_ref[...]))
    hid = jnp.sin(freq_ref[1:2, :] * (_dot_hi(hid, w2_ref[...]) + b2_ref[...]))
    t = feats[:, 0:1]
    hf = _dot_hi(hid, w3f_ref[0]) * jnp.exp(-jnp.exp(ld_ref[0, 0:1, :]) * t)
    hb = _dot_hi(hid, w3b_ref[0]) * jnp.exp(-jnp.exp(ld_ref[0, 1:2, :]) * t)
    r = lax.broadcasted_iota(jnp.int32, hf.shape, 0)
    k = jnp.where(r < n, hf, jnp.where(r > n, hb, 0.0))
    return k * lax.rsqrt(jnp.sum(k * k, axis=0, keepdims=True) + EPS)


def _filter_dense_kernel(feats_ref, w1_ref, b1_ref, w2_ref, b2_ref, w3f_ref, w3b_ref, freq_ref, ld_ref,
                         md_ref, spec_ref):
    k = _filter_taps(feats_ref, w1_ref, b1_ref, w2_ref, b2_ref, w3f_ref, w3b_ref, freq_ref, ld_ref)
    nn = k.shape[0]
    z = _dot(md_ref[...], k.astype(BF16))
    spec_ref[0, 0] = z[:nn].astype(BF16)
    spec_ref[0, 1] = z[nn:].astype(BF16)


def _fft_outer_forward(src_ref, m1_ref, bre_ref, bim_ref, k1):
    def body(s2, carry):
        xs = src_ref[pl.ds(s2, k1, stride=FFT_N2), :].astype(BF16)
        r = _dot(m1_ref[s2], xs)
        base = pl.multiple_of(s2 * FFT_N1, FFT_N1)
        bre_ref[pl.ds(base, FFT_N1), :] = r[:FFT_N1]
        bim_ref[pl.ds(base, FFT_N1), :] = r[FFT_N1:]
        return carry
    lax.fori_loop(0, FFT_N2, body, 0, unroll=FFT_UNROLL)


def _fft_inner_block(bre_ref, bim_ref, mf_ref, f1):
    xr = bre_ref[pl.ds(f1, FFT_N2, stride=FFT_N1), :]
    xi = bim_ref[pl.ds(f1, FFT_N2, stride=FFT_N1), :]
    xx = jnp.concatenate([xr, xi], axis=0).astype(BF16)
    z = _dot(mf_ref[...], xx)
    return z[:FFT_N2], z[FFT_N2:]


def _filter_fft_kernel(feats_ref, w1_ref, b1_ref, w2_ref, b2_ref, w3f_ref, w3b_ref, freq_ref, ld_ref,
                       m1_ref, mf_ref, spec_ref, k_ref, bre_ref, bim_ref):
    k_ref[...] = _filter_taps(feats_ref, w1_ref, b1_ref, w2_ref, b2_ref, w3f_ref, w3b_ref, freq_ref, ld_ref)
    _fft_outer_forward(k_ref, m1_ref, bre_ref, bim_ref, FFT_N1)

    def body(f1, carry):
        zr, zi = _fft_inner_block(bre_ref, bim_ref, mf_ref, f1)
        base = pl.multiple_of(f1 * FFT_N2, FFT_N2)
        spec_ref[0, 0, pl.ds(base, FFT_N2), :] = zr.astype(BF16)
        spec_ref[0, 1, pl.ds(base, FFT_N2), :] = zi.astype(BF16)
        return carry
    lax.fori_loop(0, FFT_N1, body, 0, unroll=FFT_UNROLL // 2)


def _hyena_spectrum(n, feats, w1, b1, w2, b2, w3, freq, ld, tables):
    nn = 2 * n
    nchunk = HY_WIDTH // LANES
    c2 = lambda o, c: (0, 0)
    in_specs = [pl.BlockSpec((nn, LANES), c2),
                pl.BlockSpec((LANES, LANES), c2), pl.BlockSpec((1, LANES), c2),
                pl.BlockSpec((LANES, LANES), c2), pl.BlockSpec((1, LANES), c2),
                pl.BlockSpec((1, LANES, LANES), lambda o, c: (o * 2 * nchunk + c, 0, 0)),
                pl.BlockSpec((1, LANES, LANES), lambda o, c: (o * 2 * nchunk + nchunk + c, 0, 0)),
                pl.BlockSpec((2, LANES), c2),
                pl.BlockSpec((1, 2, LANES), lambda o, c: (o * nchunk + c, 0, 0))]
    args = [feats, w1, b1, w2, b2, w3, w3, freq, ld]
    out_spec = pl.BlockSpec((1, 2, nn, LANES), lambda o, c: (o, 0, 0, c))
    out_shape = jax.ShapeDtypeStruct((HY_ORDER, 2, nn, HY_WIDTH), BF16)
    if "md" in tables:
        return pl.pallas_call(
            _filter_dense_kernel, grid=(HY_ORDER, nchunk),
            in_specs=in_specs + [pl.BlockSpec((2 * nn, nn), c2)],
            out_specs=out_spec, out_shape=out_shape,
            compiler_params=_cparams("parallel", "parallel"), name="hyena_filter_dense",
        )(*args, tables["md"])
    return pl.pallas_call(
        _filter_fft_kernel, grid=(HY_ORDER, nchunk),
        in_specs=in_specs + [pl.BlockSpec((FFT_N2, 2 * FFT_N1, FFT_N1), lambda o, c: (0, 0, 0)),
                             pl.BlockSpec((2 * FFT_N2, 2 * FFT_N2), c2)],
        out_specs=out_spec, out_shape=out_shape,
        scratch_shapes=[pltpu.VMEM((nn, LANES), F32), pltpu.VMEM((nn, LANES), F32), pltpu.VMEM((nn, LANES), F32)],
        compiler_params=_cparams("parallel", "parallel"), name="hyena_filter_fft",
    )(*args, tables["m1_full"], tables["mf"])


def _short_conv(u, cw, cb):
    n = u.shape[0]
    row = lax.broadcasted_iota(jnp.int32, u.shape, 0)
    prev = jnp.where(row == 0, 0.0, pltpu.roll(u, 1, 0))
    nxt = jnp.where(row == n - 1, 0.0, pltpu.roll(u, n - 1, 0))
    return prev * cw[0:1, :] + u * cw[1:2, :] + nxt * cw[2:3, :] + cb


def _hyena_dense_kernel(x1_ref, x2_ref, v_ref, cw1_ref, cw2_ref, cwv_ref, cb1_ref, cb2_ref, cbv_ref, skip_ref,
                        spec_ref, md_ref, mdi_ref, o_ref):
    n = v_ref.shape[1]
    nn = 2 * n
    z = _short_conv(v_ref[0].astype(F32), cwv_ref[...], cbv_ref[...])
    gates = ((x1_ref, cw1_ref, cb1_ref), (x2_ref, cw2_ref, cb2_ref))
    for o, (g_ref, cw_ref, cb_ref) in enumerate(gates):
        zf = _dot(md_ref[...], z.astype(BF16))
        zr, zi = zf[:nn], zf[nn:]
        sr = spec_ref[o, 0].astype(F32)
        si = spec_ref[o, 1].astype(F32)
        yy = jnp.concatenate([zr * sr - zi * si, zr * si + zi * sr], axis=0).astype(BF16)
        y = _dot(mdi_ref[...], yy) * (1.0 / nn)
        gate = _short_conv(g_ref[0].astype(F32), cw_ref[...], cb_ref[...])
        z = gate * (y + skip_ref[o:o + 1, :] * z)
    o_ref[0] = z.astype(BF16)


def _hyena_fft_kernel(x1_ref, x2_ref, v_ref, cw1_ref, cw2_ref, cwv_ref, cb1_ref, cb2_ref, cbv_ref, skip_ref,
                      spec_ref, m1_ref, mf_ref, mi_ref, m3_ref, o_ref,
                      z_ref, bre_ref, bim_ref, pre_ref, pim_ref):
    n = v_ref.shape[1]
    nn = 2 * n
    k1 = FFT_N1 // 2
    z_ref[...] = _short_conv(v_ref[0].astype(F32), cwv_ref[...], cbv_ref[...])
    gates = ((x1_ref, cw1_ref, cb1_ref), (x2_ref, cw2_ref, cb2_ref))
    for o, (g_ref, cw_ref, cb_ref) in enumerate(gates):
        _fft_outer_forward(z_ref, m1_ref, bre_ref, bim_ref, k1)

        def mid(f1, carry):
            zr, zi = _fft_inner_block(bre_ref, bim_ref, mf_ref, f1)
            base = pl.multiple_of(f1 * FFT_N2, FFT_N2)
            sr = spec_ref[o, 0, pl.ds(base, FFT_N2), :].astype(F32)
            si = spec_ref[o, 1, pl.ds(base, FFT_N2), :].astype(F32)
            yy = jnp.concatenate([zr * sr - zi * si, zr * si + zi * sr], axis=0).astype(BF16)
            p = _dot(mi_ref[...], yy)
            pre_ref[pl.ds(base, FFT_N2), :] = p[:FFT_N2]
            pim_ref[pl.ds(base, FFT_N2), :] = p[FFT_N2:]
            return carry
        lax.fori_loop(0, FFT_N1, mid, 0, unroll=FFT_UNROLL // 2)

        def last(t2, carry):
            pr = pre_ref[pl.ds(t2, FFT_N1, stride=FFT_N2), :]
            pi_ = pim_ref[pl.ds(t2, FFT_N1, stride=FFT_N2), :]
            xx = jnp.concatenate([pr, pi_], axis=0).astype(BF16)
            yv = _dot(m3_ref[t2], xx) * (1.0 / nn)
            bre_ref[pl.ds(t2, k1, stride=FFT_N2), :] = yv
            return carry
        lax.fori_loop(0, FFT_N2, last, 0, unroll=FFT_UNROLL)

        gate = _short_conv(g_ref[0].astype(F32), cw_ref[...], cb_ref[...])
        z_ref[...] = gate * (bre_ref[pl.ds(0, n), :] + skip_ref[o:o + 1, :] * z_ref[...])
    o_ref[0] = z_ref[...].astype(BF16)


def _hyena_conv(u_hy, conv_w, conv_b, skip, spec, tables):
    b, n, _ = u_hy.shape
    nn = 2 * n
    nchunk = HY_WIDTH // LANES
    cb = conv_b.reshape(1, 3 * HY_WIDTH)

    def part(p):
        return pl.BlockSpec((1, n, LANES), lambda c, bi: (bi, 0, p * nchunk + c))

    def cwp(p):
        return pl.BlockSpec((3, LANES), lambda c, bi: (0, p * nchunk + c))

    def cbp(p):
        return pl.BlockSpec((1, LANES), lambda c, bi: (0, p * nchunk + c))

    in_specs = [part(0), part(1), part(2), cwp(0), cwp(1), cwp(2), cbp(0), cbp(1), cbp(2),
                pl.BlockSpec((HY_ORDER, LANES), lambda c, bi: (0, c)),
                pl.BlockSpec((HY_ORDER, 2, nn, LANES), lambda c, bi: (0, 0, 0, c), pipeline_mode=pl.Buffered(1))]
    args = [u_hy, u_hy, u_hy, conv_w, conv_w, conv_w, cb, cb, cb, skip, spec]
    out_spec = pl.BlockSpec((1, n, LANES), lambda c, bi: (bi, 0, c))
    out_shape = jax.ShapeDtypeStruct((b, n, HY_WIDTH), BF16)
    c2 = lambda c, bi: (0, 0)
    c3 = lambda c, bi: (0, 0, 0)
    if "md" in tables:
        return pl.pallas_call(
            _hyena_dense_kernel, grid=(nchunk, b),
            in_specs=in_specs + [pl.BlockSpec((2 * nn, n), c2), pl.BlockSpec((n, 2 * nn), c2)],
            out_specs=out_spec, out_shape=out_shape,
            compiler_params=_cparams("parallel", "parallel"), name="hyena_conv_dense",
        )(*args, tables["md_half"], tables["mdi"])
    one = pl.Buffered(1)
    return pl.pallas_call(
        _hyena_fft_kernel, grid=(nchunk, b),
        in_specs=in_specs + [pl.BlockSpec((FFT_N2, 2 * FFT_N1, FFT_N1 // 2), c3, pipeline_mode=one),
                             pl.BlockSpec((2 * FFT_N2, 2 * FFT_N2), c2),
                             pl.BlockSpec((2 * FFT_N2, 2 * FFT_N2), c2),
                             pl.BlockSpec((FFT_N2, FFT_N1 // 2, 2 * FFT_N1), c3, pipeline_mode=one)],
        out_specs=out_spec, out_shape=out_shape,
        scratch_shapes=[pltpu.VMEM((n, LANES), F32)] + [pltpu.VMEM((nn, LANES), F32)] * 4,
        compiler_params=_cparams("parallel", "parallel"), name="hyena_conv_fft",
    )(*args, tables["m1_half"], tables["mf"], tables["mi"], tables["m3"])


def _fnet_kernel(u_ref, bc_ref, bs_ref, lhs_ref, o_ref, xcs_ref):
    n = u_ref.shape[1]

    @pl.when(pl.program_id(1) == 0)
    def _():
        u = u_ref[0]
        xcs_ref[pl.ds(0, n), :] = _dot(u, bc_ref[...]).astype(BF16)
        xcs_ref[pl.ds(n, n), :] = _dot(u, bs_ref[...]).astype(BF16)

    scale = 1.0 / math.sqrt(n * FN_GROUP_DIM)
    o_ref[0] = (_dot(lhs_ref[...], xcs_ref[...]) * scale).astype(BF16)


def _fnet(u_fn, tables):
    b, n, _ = u_fn.shape
    tq = min(FNET_TILE, n)
    return pl.pallas_call(
        _fnet_kernel, grid=(b, n // tq),
        in_specs=[pl.BlockSpec((1, n, FN_WIDTH), lambda bi, i: (bi, 0, 0)),
                  pl.BlockSpec((FN_WIDTH, FN_WIDTH), lambda bi, i: (0, 0)),
                  pl.BlockSpec((FN_WIDTH, FN_WIDTH), lambda bi, i: (0, 0)),
                  pl.BlockSpec((tq, 2 * n), lambda bi, i: (i, 0))],
        out_specs=pl.BlockSpec((1, tq, FN_WIDTH), lambda bi, i: (bi, i, 0)),
        out_shape=jax.ShapeDtypeStruct((b, n, FN_WIDTH), BF16),
        scratch_shapes=[pltpu.VMEM((2 * n, FN_WIDTH), BF16)],
        compiler_params=_cparams("parallel", "arbitrary"), name="fnet",
    )(u_fn, tables["bc"], tables["bs"], tables["fn_lhs"])


def _kv_kernel(ckv_ref, ks_ref, cosk_ref, wk_ref, wv_ref, k_ref, v_ref):
    ckv = ckv_ref[0].astype(BF16)
    kr = (ks_ref[0] * cosk_ref[...]).astype(BF16)
    k_ref[0] = _dot(jnp.concatenate([ckv, kr], axis=-1), wk_ref[...]).astype(BF16)
    v = _dot(ckv, wv_ref[...])
    lane = lax.broadcasted_iota(jnp.int32, v.shape, 1)
    v_ref[0] = jnp.where(lane % HEAD_PAD == V_HEAD_DIM, 1.0, v).astype(BF16)


def _kv_up(ckv_all, ks_all, cosk, wk_aug, wv):
    b, lk, _ = ckv_all.shape
    tk = ROW_TILE
    hw = N_HEADS * HEAD_PAD
    vw = N_HEADS * HEAD_PAD
    return pl.pallas_call(
        _kv_kernel, grid=(b, lk // tk),
        in_specs=[pl.BlockSpec((1, tk, KV_LORA_RANK), lambda bi, i: (bi, i, 0)),
                  pl.BlockSpec((1, tk, LANES), lambda bi, i: (bi, i, 0)),
                  pl.BlockSpec((tk, LANES), lambda bi, i: (i, 0)),
                  pl.BlockSpec((KV_LORA_RANK + LANES, hw), lambda bi, i: (0, 0)),
                  pl.BlockSpec((KV_LORA_RANK, vw), lambda bi, i: (0, 0))],
        out_specs=[pl.BlockSpec((1, tk, hw), lambda bi, i: (bi, i, 0)),
                   pl.BlockSpec((1, tk, vw), lambda bi, i: (bi, i, 0))],
        out_shape=[jax.ShapeDtypeStruct((b, lk, hw), BF16), jax.ShapeDtypeStruct((b, lk, vw), BF16)],
        compiler_params=_cparams("parallel", "parallel"), name="kv_up",
    )(ckv_all, ks_all, cosk, wk_aug, wv)


def _attn_kernel(q_ref, k_ref, v_ref, o_ref):
    outs = []
    for h in range(N_HEADS):
        q = q_ref[0, :, h * HEAD_PAD:(h + 1) * HEAD_PAD]
        k = k_ref[0, :, h * HEAD_PAD:(h + 1) * HEAD_PAD]
        s = lax.dot_general(q, k, (((1,), (1,)), ((), ())), preferred_element_type=F32)
        m = jnp.max(s, axis=-1, keepdims=True)
        p = jnp.exp2((s - m).astype(BF16))
        ov = _dot(p, v_ref[0, :, h * HEAD_PAD:(h + 1) * HEAD_PAD])
        outs.append(ov[:, :V_HEAD_DIM] * (1.0 / ov[:, V_HEAD_DIM:V_HEAD_DIM + 1]))
    o_ref[0] = jnp.concatenate(outs, axis=-1).astype(BF16)


def _attention(q, k, v):
    b, n, hw = q.shape
    lk = k.shape[1]
    vw = N_HEADS * V_HEAD_DIM
    tq = ATTN_Q_TILE
    one = pl.Buffered(1)
    return pl.pallas_call(
        _attn_kernel, grid=(b, n // tq),
        in_specs=[pl.BlockSpec((1, tq, hw), lambda bi, i: (bi, i, 0)),
                  pl.BlockSpec((1, lk, hw), lambda bi, i: (bi, 0, 0), pipeline_mode=one),
                  pl.BlockSpec((1, lk, hw), lambda bi, i: (bi, 0, 0), pipeline_mode=one)],
        out_specs=pl.BlockSpec((1, tq, vw), lambda bi, i: (bi, i, 0)),
        out_shape=jax.ShapeDtypeStruct((b, n, vw), BF16),
        compiler_params=_cparams("parallel", "parallel"), name="attention",
    )(q, k, v)


def _out_proj_kernel(hy_ref, fn_ref, att_ref, x_ref, mod_ref, wo_ref, gpost_ref, gffn_ref, rw_ref, rb_ref,
                     x1_ref, h2_ref, lg_ref):
    o1 = HY_WIDTH
    o2 = o1 + FN_WIDTH
    m = (_dot(hy_ref[0], wo_ref[:o1, :]) + _dot(fn_ref[0], wo_ref[o1:o2, :]) + _dot(att_ref[0], wo_ref[o2:, :]))
    g1 = mod_ref[0, 2:3, :]
    sh2 = mod_ref[0, 3:4, :]
    sc2 = mod_ref[0, 4:5, :]
    x1 = x_ref[0] + g1 * _rms(m, gpost_ref[...])
    x1_ref[0] = x1
    h2 = _rms(x1, gffn_ref[...]) * (1.0 + sc2) + sh2
    h2_ref[0] = h2
    lg_ref[0] = _dot_hi(h2, rw_ref[...]) + rb_ref[...]


def _out_proj(hy, fn, att, x, mod, w_out, g_post, g_ffn, rw_pad, rb_pad):
    b, n, _ = x.shape
    tm = ROW_TILE
    mix = HY_WIDTH + FN_WIDTH + N_HEADS * V_HEAD_DIM
    mod_map = (lambda bi, i: (bi, 0, 0)) if mod.shape[0] > 1 else (lambda bi, i: (0, 0, 0))
    const = lambda bi, i: (0, 0)
    row = lambda w: pl.BlockSpec((1, tm, w), lambda bi, i: (bi, i, 0))
    return pl.pallas_call(
        _out_proj_kernel, grid=(b, n // tm),
        in_specs=[row(HY_WIDTH), row(FN_WIDTH), row(N_HEADS * V_HEAD_DIM), row(D_MODEL),
                  pl.BlockSpec((1, 6, D_MODEL), mod_map),
                  pl.BlockSpec((mix, D_MODEL), const),
                  pl.BlockSpec((1, D_MODEL), const), pl.BlockSpec((1, D_MODEL), const),
                  pl.BlockSpec((D_MODEL, LANES), const), pl.BlockSpec((1, LANES), const)],
        out_specs=[row(D_MODEL), row(D_MODEL), row(LANES)],
        out_shape=[jax.ShapeDtypeStruct((b, n, D_MODEL), F32),
                   jax.ShapeDtypeStruct((b, n, D_MODEL), F32),
                   jax.ShapeDtypeStruct((b, n, LANES), F32)],
        compiler_params=_cparams("parallel", "parallel"), name="out_proj",
    )(hy, fn, att, x, mod, w_out, g_post, g_ffn, rw_pad, rb_pad)


def _route_kernel(lg_ref, info_ref, cnt_ref, carry_ref):
    i = pl.program_id(0)

    @pl.when(i == 0)
    def _():
        carry_ref[...] = jnp.zeros_like(carry_ref)

    v = lg_ref[...]
    tm = v.shape[0]
    lane = lax.broadcasted_iota(jnp.int32, v.shape, 1)
    sels, vals, idxs = [], [], []
    for _ in range(TOP_K):
        m = jnp.max(v, axis=-1, keepdims=True)
        idx = jnp.min(jnp.where(v == m, lane, LANES), axis=-1, keepdims=True)
        sel = lane == idx
        sels.append(sel)
        vals.append(m)
        idxs.append(idx)
        v = jnp.where(sel, -jnp.inf, v)
    es = [jnp.exp(val - vals[0]) for val in vals]
    den = es[0] + es[1] + es[2] + es[3]
    onehot = jnp.where(sels[0] | sels[1] | sels[2] | sels[3], 1.0, 0.0)
    r_i = lax.broadcasted_iota(jnp.int32, (tm, tm), 0)
    c_i = lax.broadcasted_iota(jnp.int32, (tm, tm), 1)
    tri = jnp.where(c_i < r_i, 1.0, 0.0).astype(BF16)
    rank = _dot(tri, onehot.astype(BF16)) + carry_ref[0:1, :]
    info = jnp.zeros(v.shape, F32)
    for kk in range(TOP_K):
        rk = jnp.sum(jnp.where(sels[kk], rank, 0.0), axis=-1, keepdims=True)
        info = jnp.where(lane == kk, es[kk] / den, info)
        info = jnp.where(lane == TOP_K + kk, idxs[kk].astype(F32), info)
        info = jnp.where(lane == 2 * TOP_K + kk, rk, info)
    info_ref[...] = info
    carry_ref[0:1, :] = carry_ref[0:1, :] + jnp.sum(onehot, axis=0, keepdims=True)
    cnt_ref[...] = carry_ref[...]


def _route(logits):
    t = logits.shape[0]
    tm = ROUTE_TILE
    return pl.pallas_call(
        _route_kernel, grid=(t // tm,),
        in_specs=[pl.BlockSpec((tm, LANES), lambda i: (i, 0))],
        out_specs=[pl.BlockSpec((tm, LANES), lambda i: (i, 0)), pl.BlockSpec((8, LANES), lambda i: (0, 0))],
        out_shape=[jax.ShapeDtypeStruct((t, LANES), F32), jax.ShapeDtypeStruct((8, LANES), F32)],
        scratch_shapes=[pltpu.VMEM((8, LANES), F32)],
        compiler_params=_cparams("arbitrary"), name="route",
    )(logits)


def _dispatch_kernel(dest_ref, h_ref, xb_in_ref, xb_ref, idx_ref, isem, sem):
    del xb_in_ref
    i = pl.program_id(0)
    rows = h_ref.shape[0]
    cp = pltpu.make_async_copy(dest_ref.at[i], idx_ref, isem)
    cp.start()
    cp.wait()

    def row_copy(r, kk, slot):
        return pltpu.make_async_copy(h_ref.at[pl.ds(r, 1)], xb_ref.at[pl.ds(slot, 1)], sem)

    def issue(r, carry):
        for kk in range(TOP_K):
            row_copy(r, kk, idx_ref[r * TOP_K + kk]).start()
        return carry
    lax.fori_loop(0, rows, issue, 0)

    def drain(r, carry):
        for kk in range(TOP_K):
            row_copy(r, kk, 0).wait()
        return carry
    lax.fori_loop(0, rows, drain, 0)


def _dispatch(h2, dest, n_slots):
    t = h2.shape[0]
    rows = DMA_ROWS
    xb0 = jnp.zeros((n_slots, D_MODEL), F32)
    return pl.pallas_call(
        _dispatch_kernel, grid=(t // rows,),
        in_specs=[pl.BlockSpec(memory_space=pl.ANY),
                  pl.BlockSpec((rows, D_MODEL), lambda i: (i, 0)),
                  pl.BlockSpec(memory_space=pl.ANY)],
        out_specs=pl.BlockSpec(memory_space=pl.ANY),
        out_shape=jax.ShapeDtypeStruct((n_slots, D_MODEL), F32),
        scratch_shapes=[pltpu.SMEM((rows * TOP_K,), jnp.int32),
                        pltpu.SemaphoreType.DMA(()), pltpu.SemaphoreType.DMA(())],
        input_output_aliases={2: 0},
        compiler_params=_cparams("arbitrary"), name="moe_dispatch",
    )(dest.reshape(t // rows, rows * TOP_K), h2, xb0)


def _deinterleave_kernel(w_ref, s_ref, o_ref):
    for k in range(w_ref.shape[2] // MXU_TILE):
        blk = w_ref[0, :, k * MXU_TILE:(k + 1) * MXU_TILE].astype(BF16)
        o_ref[0, :, k * MXU_TILE:(k + 1) * MXU_TILE] = _dot(blk, s_ref[...]).astype(BF16)


def _deinterleave_gate_up(w_gu):
    ne, d, f2 = w_gu.shape
    tr = 512
    half = MXU_TILE // 2
    sel = np.zeros((MXU_TILE, MXU_TILE), np.float32)
    for j in range(half):
        sel[2 * j, j] = 1.0
        sel[2 * j + 1, half + j] = 1.0
    return pl.pallas_call(
        _deinterleave_kernel, grid=(ne, d // tr),
        in_specs=[pl.BlockSpec((1, tr, f2), lambda e, i: (e, i, 0)),
                  pl.BlockSpec((MXU_TILE, MXU_TILE), lambda e, i: (0, 0))],
        out_specs=pl.BlockSpec((1, tr, f2), lambda e, i: (e, i, 0)),
        out_shape=jax.ShapeDtypeStruct((ne, d, f2), BF16),
        compiler_params=_cparams("parallel", "parallel"), name="moe_weight_prep",
    )(w_gu, jnp.asarray(sel, BF16))


def _expert_kernel(be_ref, nu_ref, x_ref, wgu_ref, bgu_ref, wd_ref, bd_ref, o_ref):
    i = pl.program_id(0)

    @pl.when(i < nu_ref[0])
    def _():
        x = x_ref[...].astype(BF16)
        gu = _dot(x, wgu_ref[0]) + bgu_ref[0]
        half = MXU_TILE // 2
        acts = []
        for k in range(gu.shape[1] // MXU_TILE):
            g = jnp.minimum(gu[:, k * MXU_TILE:k * MXU_TILE + half], SWIGLU_LIMIT)
            lin = jnp.clip(gu[:, k * MXU_TILE + half:(k + 1) * MXU_TILE], -SWIGLU_LIMIT, SWIGLU_LIMIT)
            acts.append((g * jax.nn.sigmoid(SWIGLU_ALPHA * g) * (lin + 1.0)).astype(BF16))
        o_ref[...] = _dot(jnp.concatenate(acts, axis=-1), wd_ref[0]) + bd_ref[0]

    @pl.when(i >= nu_ref[0])
    def _():
        o_ref[...] = jnp.zeros_like(o_ref)


def _experts(xb, block_exp, n_used, wgu, bgu, wd, bd):
    n_slots = xb.shape[0]
    bm = MOE_BLOCK
    wmap = lambda i, be, nu: (be[i], 0, 0)
    return pl.pallas_call(
        _expert_kernel,
        grid_spec=pltpu.PrefetchScalarGridSpec(
            num_scalar_prefetch=2, grid=(n_slots // bm,),
            in_specs=[pl.BlockSpec((bm, D_MODEL), lambda i, be, nu: (i, 0)),
                      pl.BlockSpec((1, D_MODEL, 2 * D_FF), wmap), pl.BlockSpec((1, 1, 2 * D_FF), wmap),
                      pl.BlockSpec((1, D_FF, D_MODEL), wmap), pl.BlockSpec((1, 1, D_MODEL), wmap)],
            out_specs=pl.BlockSpec((bm, D_MODEL), lambda i, be, nu: (i, 0))),
        out_shape=jax.ShapeDtypeStruct((n_slots, D_MODEL), F32),
        compiler_params=_cparams("arbitrary"), name="moe_experts",
    )(block_exp, n_used, xb, wgu, bgu, wd, bd)


def _combine_kernel(dest_ref, ys_ref, info_ref, x1_ref, mod_ref, gpost_ref, o_ref, idx_ref, buf_ref, isem, sem):
    i = pl.program_id(1)
    nb = pl.num_programs(1)
    rows = x1_ref.shape[1]
    cp = pltpu.make_async_copy(dest_ref.at[pl.program_id(0) * nb + i], idx_ref, isem)
    cp.start()
    cp.wait()

    def row_copy(r, kk, slot):
        return pltpu.make_async_copy(ys_ref.at[pl.ds(slot, 1)], buf_ref.at[kk, pl.ds(r, 1)], sem)

    def issue(r, carry):
        for kk in range(TOP_K):
            row_copy(r, kk, idx_ref[r * TOP_K + kk]).start()
        return carry
    lax.fori_loop(0, rows, issue, 0)

    def drain(r, carry):
        for kk in range(TOP_K):
            row_copy(r, kk, 0).wait()
        return carry
    lax.fori_loop(0, rows, drain, 0)

    info = info_ref[...]
    f = info[:, 0:1] * buf_ref[0]
    for kk in range(1, TOP_K):
        f = f + info[:, kk:kk + 1] * buf_ref[kk]
    g2 = mod_ref[0, 5:6, :]
    o_ref[0] = x1_ref[0] + g2 * _rms(f, gpost_ref[...])


def _combine(ys, dest, info, x1, mod, g_post):
    b, n, _ = x1.shape
    rows = DMA_ROWS
    nb = n // rows
    mod_map = (lambda bi, i: (bi, 0, 0)) if mod.shape[0] > 1 else (lambda bi, i: (0, 0, 0))
    return pl.pallas_call(
        _combine_kernel, grid=(b, nb),
        in_specs=[pl.BlockSpec(memory_space=pl.ANY),
                  pl.BlockSpec(memory_space=pl.ANY),
                  pl.BlockSpec((rows, LANES), lambda bi, i: (bi * nb + i, 0)),
                  pl.BlockSpec((1, rows, D_MODEL), lambda bi, i: (bi, i, 0)),
                  pl.BlockSpec((1, 6, D_MODEL), mod_map),
                  pl.BlockSpec((1, D_MODEL), lambda bi, i: (0, 0))],
        out_specs=pl.BlockSpec((1, rows, D_MODEL), lambda bi, i: (bi, i, 0)),
        out_shape=jax.ShapeDtypeStruct((b, n, D_MODEL), F32),
        scratch_shapes=[pltpu.SMEM((rows * TOP_K,), jnp.int32),
                        pltpu.VMEM((TOP_K, rows, D_MODEL), F32),
                        pltpu.SemaphoreType.DMA(()), pltpu.SemaphoreType.DMA(())],
        compiler_params=_cparams("arbitrary", "arbitrary"), name="moe_combine",
    )(dest.reshape(b * nb, rows * TOP_K), ys, info, x1, mod, g_post)


def _moe_and_residual(h2, logits, x1, mod, g_post, ew):
    b, n, _ = x1.shape
    t = b * n
    info, cnt = _route(logits.reshape(t, LANES))
    counts = cnt[0, :N_EXPERTS].astype(jnp.int32)
    bm = MOE_BLOCK
    padded = (counts + bm - 1) // bm * bm
    pad_end = jnp.cumsum(padded)
    pad_start = pad_end - padded
    e_idx = info[:, TOP_K:2 * TOP_K].astype(jnp.int32)
    rank = info[:, 2 * TOP_K:3 * TOP_K].astype(jnp.int32)
    ids = jnp.arange(N_EXPERTS, dtype=jnp.int32)
    start_of = jnp.sum(jnp.where(e_idx[:, :, None] == ids, pad_start, 0), axis=-1)
    dest = (start_of + rank).reshape(-1)
    n_blocks = t * TOP_K // bm + N_EXPERTS
    n_slots = n_blocks * bm
    first_row = jnp.arange(n_blocks, dtype=jnp.int32) * bm
    block_exp = jnp.minimum(jnp.sum((pad_end[None, :] <= first_row[:, None]).astype(jnp.int32), axis=-1),
                            N_EXPERTS - 1)
    n_used = (pad_end[-1:] // bm).astype(jnp.int32)
    xb = _dispatch(h2.reshape(t, D_MODEL), dest, n_slots)
    ys = _experts(xb, block_exp, n_used, *ew)
    return _combine(ys, dest, info, x1, mod, g_post)


def _cos_sin(num, den, shape_like=None):
    ang = (2.0 * math.pi / den) * num.astype(F32)
    return jnp.cos(ang), jnp.sin(ang)


def _dense_dft_tables(n):
    nn = 2 * n
    f = jnp.arange(nn, dtype=jnp.int32)[:, None]
    s = jnp.arange(nn, dtype=jnp.int32)[None, :]
    c, sn = _cos_sin((f * s) % nn, nn)
    md = jnp.concatenate([c, -sn], axis=0).astype(BF16)
    ci, si = c[:n, :], sn[:n, :]
    mdi = jnp.concatenate([ci, -si], axis=1).astype(BF16)
    return {"md": md, "md_half": md[:, :n], "mdi": mdi}


def _fft_tables(n):
    nn = 2 * n
    n1, n2 = FFT_N1, FFT_N2
    assert nn == n1 * n2
    s2 = jnp.arange(n2, dtype=jnp.int32)[:, None, None]
    f1 = jnp.arange(n1, dtype=jnp.int32)[None, :, None]
    s1 = jnp.arange(n1, dtype=jnp.int32)[None, None, :]
    c, sn = _cos_sin((f1 * (n2 * s1 + s2)) % nn, nn)
    m1 = jnp.concatenate([c, -sn], axis=1).astype(BF16)
    a = jnp.arange(n2, dtype=jnp.int32)
    c2, sn2 = _cos_sin((a[:, None] * a[None, :]) % n2, n2)
    mf = jnp.concatenate([jnp.concatenate([c2, sn2], axis=1),
                          jnp.concatenate([-sn2, c2], axis=1)], axis=0).astype(BF16)
    mi = jnp.concatenate([jnp.concatenate([c2, -sn2], axis=1),
                          jnp.concatenate([sn2, c2], axis=1)], axis=0).astype(BF16)
    t2 = jnp.arange(n2, dtype=jnp.int32)[:, None, None]
    t1 = jnp.arange(n1 // 2, dtype=jnp.int32)[None, :, None]
    g1 = jnp.arange(n1, dtype=jnp.int32)[None, None, :]
    c3, sn3 = _cos_sin((g1 * (n2 * t1 + t2)) % nn, nn)
    m3 = jnp.concatenate([c3, -sn3], axis=2).astype(BF16)
    return {"m1_full": m1, "m1_half": m1[:, :, :n1 // 2], "mf": mf, "mi": mi, "m3": m3}


def _fnet_tables(n):
    j = jnp.arange(FN_WIDTH, dtype=jnp.int32)
    same = (j[:, None] // FN_GROUP_DIM) == (j[None, :] // FN_GROUP_DIM)
    c, s = _cos_sin(((j[:, None] % FN_GROUP_DIM) * (j[None, :] % FN_GROUP_DIM)) % FN_GROUP_DIM, FN_GROUP_DIM)
    bc = jnp.where(same, c, 0.0).astype(BF16)
    bs = jnp.where(same, s, 0.0).astype(BF16)
    p = jnp.arange(n, dtype=jnp.int32)
    cl, sl = _cos_sin((p[:, None] * p[None, :]) % n, n)
    return {"bc": bc, "bs": bs, "fn_lhs": jnp.concatenate([cl, -sl], axis=1).astype(BF16)}


def _filter_feats(n):
    r = jnp.arange(2 * n, dtype=jnp.int32)
    pos = jnp.where(r < n, r, 2 * n - r).astype(F32)
    t = pos / n
    bands = jnp.linspace(1e-4, FILT_BANDS - 1, FILT_BANDS, dtype=F32)
    ang = (2.0 * math.pi / n) * pos[:, None] * bands[None, :]
    feats = jnp.concatenate([t[:, None], jnp.cos(ang), -jnp.sin(ang)], axis=-1)
    return jnp.pad(feats, ((0, 0), (0, LANES - FILT_EMB)))


def _rope_tables(n, use_rope):
    if use_rope:
        tpos = jnp.arange(n, dtype=jnp.int32)
        rows = (tpos // GRID_W).astype(F32)
        cols = (tpos % GRID_W).astype(F32)
        inv = ROPE_BASE ** (-jnp.arange(AXIS_PAIRS, dtype=F32) / AXIS_PAIRS)
        ang = jnp.stack([rows[:, None] * inv, cols[:, None] * inv], axis=1)
        cos = jnp.broadcast_to(jnp.cos(ang)[:, :, None, :], (n, 2, 2, AXIS_PAIRS)).reshape(n, ROPE_DIM)
        sin = jnp.broadcast_to(jnp.sin(ang)[:, :, None, :], (n, 2, 2, AXIS_PAIRS)).reshape(n, ROPE_DIM)
    else:
        cos = jnp.ones((n, ROPE_DIM), F32)
        sin = jnp.zeros((n, ROPE_DIM), F32)
    pad = HEAD_PAD - QK_NOPE_DIM - ROPE_DIM
    cos_h = jnp.concatenate([jnp.ones((n, QK_NOPE_DIM), F32), cos, jnp.zeros((n, pad), F32)], axis=1)
    sin_h = jnp.concatenate([jnp.zeros((n, QK_NOPE_DIM), F32), sin, jnp.zeros((n, pad), F32)], axis=1)
    qscale = MLA_SCALE * math.log2(math.e)
    cosf = jnp.tile(cos_h, (1, N_HEADS)) * qscale
    sinf = jnp.tile(sin_h, (1, N_HEADS)) * qscale
    cosk = jnp.concatenate([cos, sin, jnp.zeros((n, LANES - 2 * ROPE_DIM), F32)], axis=1)
    return cosf, sinf, cosk


def _rot_matrix():
    r = np.zeros((ROPE_DIM, ROPE_DIM), np.float32)
    for a in range(2):
        for p in range(AXIS_PAIRS):
            lo = a * 2 * AXIS_PAIRS + p
            hi = lo + AXIS_PAIRS
            r[hi, lo] = -1.0
            r[lo, hi] = 1.0
    return jnp.asarray(r)


def _prep_layer(l, w):
    rot = _rot_matrix()
    w_in = w["w_in"][l]
    o4 = 3 * HY_WIDTH + FN_WIDTH + Q_LORA_RANK + KV_LORA_RANK
    w_kr = w_in[:, o4:]
    w_all = jnp.concatenate(
        [w_in[:, :o4], w_kr, w_kr @ rot, jnp.zeros((D_MODEL, LANES - 2 * ROPE_DIM), F32)], axis=1).astype(BF16)
    hd = QK_NOPE_DIM + ROPE_DIM
    pad = HEAD_PAD - hd
    wq = w["w_uq"][l].reshape(Q_LORA_RANK, N_HEADS, hd)
    zq = jnp.zeros((Q_LORA_RANK, N_HEADS, pad), F32)
    plain = jnp.concatenate([wq, zq], axis=2)
    rotated = jnp.concatenate([jnp.zeros((Q_LORA_RANK, N_HEADS, QK_NOPE_DIM), F32),
                               jnp.einsum("rhd,de->rhe", wq[:, :, QK_NOPE_DIM:], rot), zq], axis=2)
    hw = N_HEADS * HEAD_PAD
    w_uq_aug = jnp.concatenate([plain.reshape(Q_LORA_RANK, hw), rotated.reshape(Q_LORA_RANK, hw)], axis=1).astype(BF16)
    wkv = w["w_ukv"][l].reshape(KV_LORA_RANK, N_HEADS, QK_NOPE_DIM + V_HEAD_DIM)
    wk = jnp.concatenate([wkv[:, :, :QK_NOPE_DIM],
                          jnp.zeros((KV_LORA_RANK, N_HEADS, HEAD_PAD - QK_NOPE_DIM), F32)], axis=2).reshape(KV_LORA_RANK, hw)
    sel = np.zeros((LANES, N_HEADS, HEAD_PAD), np.float32)
    for j in range(ROPE_DIM):
        sel[j, :, QK_NOPE_DIM + j] = 1.0
        sel[ROPE_DIM + j, :, QK_NOPE_DIM + j] = 1.0
    wk_aug = jnp.concatenate([wk, jnp.asarray(sel).reshape(LANES, hw)], axis=0).astype(BF16)
    wv = jnp.concatenate([wkv[:, :, QK_NOPE_DIM:],
                          jnp.zeros((KV_LORA_RANK, N_HEADS, HEAD_PAD - V_HEAD_DIM), F32)], axis=2)
    wv = wv.reshape(KV_LORA_RANK, hw).astype(BF16)
    padh = LANES - FILT_HID
    w1 = jnp.pad(w["filt_w1"][l], ((0, LANES - FILT_EMB), (0, padh)))
    b1 = jnp.pad(w["filt_b1"][l], (0, padh)).reshape(1, LANES)
    w2 = jnp.pad(w["filt_w2"][l], ((0, padh), (0, padh)))
    b2 = jnp.pad(w["filt_b2"][l], (0, padh)).reshape(1, LANES)
    nblk = HY_ORDER * 2 * HY_WIDTH // LANES
    w3 = jnp.pad(w["filt_w3"][l], ((0, padh), (0, 0))).reshape(LANES, nblk, LANES).transpose(1, 0, 2)
    freq = jnp.pad(w["filt_freq"][l], ((0, 0), (0, padh)))
    nch = HY_WIDTH // LANES
    ld = w["hy_log_decay"][l].reshape(HY_ORDER, 2, nch, LANES).transpose(0, 2, 1, 3).reshape(HY_ORDER * nch, 2, LANES)
    rw = jnp.pad(w["router_w"][l], ((0, 0), (0, LANES - N_EXPERTS)))
    rb = jnp.pad(w["router_b"][l], (0, LANES - N_EXPERTS), constant_values=-1e30).reshape(1, LANES)
    half = MXU_TILE // 2
    bgu = w["moe_b_gu"][l].reshape(N_EXPERTS, 2 * D_FF // MXU_TILE, half, 2)
    bgu = bgu.transpose(0, 1, 3, 2).reshape(N_EXPERTS, 1, 2 * D_FF)
    ew = (w["moe_w_gu_prepped"][l], bgu, w["moe_w_down"][l].astype(BF16), w["moe_b_down"][l][:, None, :])
    return {
        "w_all": w_all, "w_uq_aug": w_uq_aug, "wk_aug": wk_aug, "wv": wv,
        "g_pre": w["g_pre_mix"][l].reshape(1, D_MODEL), "g_post": w["g_post_mix"][l].reshape(1, D_MODEL),
        "g_ffn": w["g_pre_ffn"][l].reshape(1, D_MODEL), "g_post_ffn": w["g_post_ffn"][l].reshape(1, D_MODEL),
        "g_q": w["q_norm_g"][l].reshape(1, Q_LORA_RANK), "g_kv": w["kv_norm_g"][l].reshape(1, KV_LORA_RANK),
        "conv_w": w["conv_w"][l], "conv_b": w["conv_b"][l], "skip": w["hy_skip"][l],
        "filt": (w1, b1, w2, b2, w3, freq, ld),
        "w_out": w["w_out"][l].astype(BF16), "rw": rw, "rb": rb, "ew": ew,
    }


def _trunk_layer(x, mod, p, pos, ctx):
    b, n, _ = x.shape
    u_hy, u_fn, q, ckv, ks = _in_proj(x, mod, p["g_pre"], p["w_all"], p["g_q"], p["w_uq_aug"], p["g_kv"],
                                      pos["cosf"], pos["sinf"])
    spec = _hyena_spectrum(n, pos["feats"], *p["filt"], pos["hy"])
    hy = _hyena_conv(u_hy, p["conv_w"], p["conv_b"], p["skip"], spec, pos["hy"])
    fn = _fnet(u_fn, pos["fn"])
    if ctx is None:
        ckv_all, ks_all, cosk = ckv, ks, pos["cosk"]
    else:
        c_ckv, c_kr = ctx
        past = c_ckv.shape[1]
        ckv_all = jnp.concatenate([c_ckv, ckv], axis=1)
        ks_all = jnp.concatenate([jnp.pad(c_kr, ((0, 0), (0, 0), (0, LANES - ROPE_DIM))), ks], axis=1)
        ident = jnp.concatenate([jnp.ones((past, ROPE_DIM), F32), jnp.zeros((past, LANES - ROPE_DIM), F32)], axis=1)
        cosk = jnp.concatenate([ident, pos["cosk"]], axis=0)
    k, v = _kv_up(ckv_all, ks_all, cosk, p["wk_aug"], p["wv"])
    att = _attention(q, k, v)
    x1, h2, logits = _out_proj(hy, fn, att, x, mod, p["w_out"], p["g_post"], p["g_ffn"], p["rw"], p["rb"])
    x2 = _moe_and_residual(h2, logits, x1, mod, p["g_post_ffn"], p["ew"])
    return x2, ckv, ks[..., :ROPE_DIM]


def _position_tables(n, use_rope, dense):
    cosf, sinf, cosk = _rope_tables(n, use_rope)
    return {"cosf": cosf, "sinf": sinf, "cosk": cosk, "feats": _filter_feats(n),
            "hy": _dense_dft_tables(n) if dense else _fft_tables(n), "fn": _fnet_tables(n)}


def kernel(x_prompt, x_sample, cache_ckv, cache_krope, c, c_ctx, w_mod, b_mod, g_pre_mix, g_post_mix, g_pre_ffn, g_post_ffn, w_in, conv_w, conv_b, filt_w1, filt_b1, filt_w2, filt_b2, filt_w3, filt_freq, hy_log_decay, hy_skip, q_norm_g, w_uq, kv_norm_g, w_ukv, w_out, router_w, router_b, moe_w_gu, moe_b_gu, moe_w_down, moe_b_down):
    w = {"g_pre_mix": g_pre_mix, "g_post_mix": g_post_mix, "g_pre_ffn": g_pre_ffn, "g_post_ffn": g_post_ffn,
         "w_in": w_in, "conv_w": conv_w, "conv_b": conv_b, "filt_w1": filt_w1, "filt_b1": filt_b1,
         "filt_w2": filt_w2, "filt_b2": filt_b2, "filt_w3": filt_w3, "filt_freq": filt_freq,
         "hy_log_decay": hy_log_decay, "hy_skip": hy_skip, "q_norm_g": q_norm_g, "w_uq": w_uq,
         "kv_norm_g": kv_norm_g, "w_ukv": w_ukv, "w_out": w_out, "router_w": router_w, "router_b": router_b,
         "moe_w_gu": moe_w_gu, "moe_b_gu": moe_b_gu, "moe_w_down": moe_w_down, "moe_b_down": moe_b_down}
    w["moe_w_gu_prepped"] = _deinterleave_gate_up(
        moe_w_gu.reshape(DEPTH * N_EXPERTS, D_MODEL, 2 * D_FF)).reshape(DEPTH, N_EXPERTS, D_MODEL, 2 * D_FF)
    nb = c.shape[0]
    rows = 16
    cc = jnp.concatenate([c_ctx[None, :], c, jnp.zeros((rows - 1 - nb, D_MODEL), F32)], axis=0)
    mod_all = _modulation(cc, w_mod, b_mod)
    layers = [_prep_layer(l, w) for l in range(DEPTH)]
    pos_p = _position_tables(x_prompt.shape[1], False, True)
    pos_s = _position_tables(x_sample.shape[1], True, False)

    y_prompt = x_prompt
    ckvs, krs = [], []
    for l in range(DEPTH):
        mod = mod_all[l, 0:1].reshape(1, 6, D_MODEL)
        y_prompt, ckv_l, kr_l = _trunk_layer(y_prompt, mod, layers[l], pos_p, None)
        ckvs.append(ckv_l)
        krs.append(kr_l)

    y_sample = x_sample
    for l in range(DEPTH):
        mod = mod_all[l, 1:1 + nb].reshape(nb, 6, D_MODEL)
        y_sample, _, _ = _trunk_layer(y_sample, mod, layers[l], pos_s, (cache_ckv[:, l], cache_krope[:, l]))

    return (y_prompt, y_sample, jnp.stack(ckvs, axis=1), jnp.stack(krs, axis=1))
```

```python
import functools
import math

import numpy as np
import jax
import jax.numpy as jnp
from jax import lax
from jax.experimental import pallas as pl
from jax.experimental.pallas import tpu as pltpu

F32 = jnp.float32
BF16 = jnp.bfloat16

D_MODEL = 1024
DEPTH = 2
GRID_W = 64
HY_WIDTH = 256
HY_ORDER = 2
FILT_BANDS = 16
FILT_EMB = 2 * FILT_BANDS + 1
FILT_HID = 64
FN_WIDTH = 256
FN_GROUP_DIM = 64
N_HEADS = 8
QK_NOPE_DIM = 64
ROPE_DIM = 32
V_HEAD_DIM = 64
Q_LORA_RANK = 256
KV_LORA_RANK = 128
MLA_SCALE = (QK_NOPE_DIM + ROPE_DIM) ** -0.5
AXIS_PAIRS = ROPE_DIM // 4
ROPE_BASE = 10000.0
N_EXPERTS = 32
TOP_K = 4
D_FF = 1024
SWIGLU_LIMIT = 7.0
SWIGLU_ALPHA = 1.702
EPS = 1e-6

LANES = 128
VMEM_LIMIT = 56 * 1024 * 1024
ROW_TILE = 256
ATTN_Q_TILE = 256
ROUTE_TILE = 512
MOE_BLOCK = 256
DMA_ROWS = 512
DMA_UNROLL = 4
FNET_TILE = 512
HEAD_PAD = 128
FFT_N1 = 64
FFT_N2 = 128
FFT_UNROLL = 8
FN_RADIX = 64
MXU_TILE = 256
IN_SLAB = 3 * HY_WIDTH + FN_WIDTH + Q_LORA_RANK + KV_LORA_RANK + LANES


def _cparams(*sem):
    return pltpu.CompilerParams(dimension_semantics=tuple(sem), vmem_limit_bytes=VMEM_LIMIT)


def _rms(x, g):
    return x * lax.rsqrt(jnp.mean(x * x, axis=-1, keepdims=True) + EPS) * g


def _dot(a, b):
    return jnp.dot(a, b, preferred_element_type=F32)


def _dot_hi(a, b):
    return jnp.dot(a, b, preferred_element_type=F32, precision=lax.Precision.HIGHEST)


def _mod_kernel(c_ref, w_ref, b_ref, o_ref):
    c = c_ref[...]
    s = c * jax.nn.sigmoid(c)
    o_ref[0] = _dot_hi(s, w_ref[0]) + b_ref[0]


def _modulation(cc, w_mod, b_mod):
    r = cc.shape[0]
    tn = 1536
    return pl.pallas_call(
        _mod_kernel,
        grid=(DEPTH, 6 * D_MODEL // tn),
        in_specs=[pl.BlockSpec((r, D_MODEL), lambda l, j: (0, 0)),
                  pl.BlockSpec((1, D_MODEL, tn), lambda l, j: (l, 0, j)),
                  pl.BlockSpec((1, 1, tn), lambda l, j: (l, 0, j))],
        out_specs=pl.BlockSpec((1, r, tn), lambda l, j: (l, 0, j)),
        out_shape=jax.ShapeDtypeStruct((DEPTH, r, 6 * D_MODEL), F32),
        compiler_params=_cparams("parallel", "parallel"),
        name="modulation",
    )(cc, w_mod, b_mod.reshape(DEPTH, 1, 6 * D_MODEL))


def _in_proj_kernel(x_ref, mod_ref, gpre_ref, win_ref, gq_ref, wuq_ref, gkv_ref, cosf_ref, sinf_ref,
                    hy_ref, fn_ref, q_ref, ckv_ref, ks_ref):
    x = x_ref[0]
    sh1 = mod_ref[0, 0:1, :]
    sc1 = mod_ref[0, 1:2, :]
    h = _rms(x, gpre_ref[...]) * (1.0 + sc1) + sh1
    proj = _dot(h.astype(BF16), win_ref[...])
    o1 = 3 * HY_WIDTH
    o2 = o1 + FN_WIDTH
    o3 = o2 + Q_LORA_RANK
    o4 = o3 + KV_LORA_RANK
    hy_ref[0] = proj[:, :o1].astype(BF16)
    fn_ref[0] = proj[:, o1:o2].astype(BF16)
    qlat = _rms(proj[:, o2:o3], gq_ref[...]).astype(BF16)
    q2 = _dot(qlat, wuq_ref[...])
    hw = N_HEADS * HEAD_PAD
    q = q2[:, :hw] * cosf_ref[...] + q2[:, hw:] * sinf_ref[...]
    q_ref[0] = q.astype(BF16)
    ckv_ref[0] = _rms(proj[:, o3:o4], gkv_ref[...])
    ks_ref[0] = proj[:, o4:]


def _in_proj(x, mod, g_pre, w_all, g_q, w_uq_aug, g_kv, cosf, sinf):
    b, n, _ = x.shape
    tm = ROW_TILE
    hw = N_HEADS * HEAD_PAD
    mod_map = (lambda bi, i: (bi, 0, 0)) if mod.shape[0] > 1 else (lambda bi, i: (0, 0, 0))
    const = lambda bi, i: (0, 0)
    return pl.pallas_call(
        _in_proj_kernel,
        grid=(b, n // tm),
        in_specs=[pl.BlockSpec((1, tm, D_MODEL), lambda bi, i: (bi, i, 0)),
                  pl.BlockSpec((1, 6, D_MODEL), mod_map),
                  pl.BlockSpec((1, D_MODEL), const),
                  pl.BlockSpec((D_MODEL, IN_SLAB), const),
                  pl.BlockSpec((1, Q_LORA_RANK), const),
                  pl.BlockSpec((Q_LORA_RANK, 2 * hw), const),
                  pl.BlockSpec((1, KV_LORA_RANK), const),
                  pl.BlockSpec((tm, hw), lambda bi, i: (i, 0)),
                  pl.BlockSpec((tm, hw), lambda bi, i: (i, 0))],
        out_specs=[pl.BlockSpec((1, tm, 3 * HY_WIDTH), lambda bi, i: (bi, i, 0)),
                   pl.BlockSpec((1, tm, FN_WIDTH), lambda bi, i: (bi, i, 0)),
                   pl.BlockSpec((1, tm, hw), lambda bi, i: (bi, i, 0)),
                   pl.BlockSpec((1, tm, KV_LORA_RANK), lambda bi, i: (bi, i, 0)),
                   pl.BlockSpec((1, tm, LANES), lambda bi, i: (bi, i, 0))],
        out_shape=[jax.ShapeDtypeStruct((b, n, 3 * HY_WIDTH), BF16),
                   jax.ShapeDtypeStruct((b, n, FN_WIDTH), BF16),
                   jax.ShapeDtypeStruct((b, n, hw), BF16),
                   jax.ShapeDtypeStruct((b, n, KV_LORA_RANK), F32),
                   jax.ShapeDtypeStruct((b, n, LANES), F32)],
        compiler_params=_cparams("parallel", "parallel"),
        name="in_proj",
    )(x, mod, g_pre, w_all, g_q, w_uq_aug, g_kv, cosf, sinf)


def _filter_taps(feats_ref, w1_ref, b1_ref, w2_ref, b2_ref, w3f_ref, w3b_ref, freq_ref, ld_ref, hid_ref):
    rows = feats_ref.shape[0]
    n = rows // 2

    @pl.when((pl.program_id(0) == 0) & (pl.program_id(1) == 0))
    def _():
        tr = min(rows, 1024)

        def chunk(i, carry):
            r0 = pl.multiple_of(i * tr, tr)
            h1 = jnp.sin(freq_ref[0:1, :] * (_dot_hi(feats_ref[pl.ds(r0, tr), :], w1_ref[...]) + b1_ref[...]))
            hid_ref[pl.ds(r0, tr), :] = jnp.sin(freq_ref[1:2, :] * (_dot_hi(h1, w2_ref[...]) + b2_ref[...]))
            return carry
        lax.fori_loop(0, rows // tr, chunk, 0)

    hf = (_dot_hi(hid_ref[pl.ds(0, n), :], w3f_ref[0])
          * jnp.exp(-jnp.exp(ld_ref[0, 0:1, :]) * feats_ref[pl.ds(0, n), 0:1]))
    hb = (_dot_hi(hid_ref[pl.ds(n, n), :], w3b_ref[0])
          * jnp.exp(-jnp.exp(ld_ref[0, 1:2, :]) * feats_ref[pl.ds(n, n), 0:1]))
    hb = jnp.where(lax.broadcasted_iota(jnp.int32, hb.shape, 0) == 0, 0.0, hb)
    ss = jnp.sum(hf * hf, axis=0, keepdims=True) + jnp.sum(hb * hb, axis=0, keepdims=True)
    inv = lax.rsqrt(ss + EPS)
    return hf * inv, hb * inv


def _filter_dense_kernel(feats_ref, w1_ref, b1_ref, w2_ref, b2_ref, w3f_ref, w3b_ref, freq_ref, ld_ref,
                         md_ref, spec_ref, hid_ref):
    kf, kb = _filter_taps(feats_ref, w1_ref, b1_ref, w2_ref, b2_ref, w3f_ref, w3b_ref, freq_ref, ld_ref, hid_ref)
    k = jnp.concatenate([kf, kb], axis=0)
    nn = k.shape[0]
    z = _dot(md_ref[...], k.astype(BF16))
    spec_ref[0, 0] = z[:nn].astype(BF16)
    spec_ref[0, 1] = z[nn:].astype(BF16)


def _fft_outer_forward(src_ref, m1_ref, bre_ref, bim_ref, k1):
    def body(s2, carry):
        xs = src_ref[pl.ds(s2, k1, stride=FFT_N2), :].astype(BF16)
        r = _dot(m1_ref[s2], xs)
        base = pl.multiple_of(s2 * FFT_N1, FFT_N1)
        bre_ref[pl.ds(base, FFT_N1), :] = r[:FFT_N1]
        bim_ref[pl.ds(base, FFT_N1), :] = r[FFT_N1:]
        return carry
    lax.fori_loop(0, FFT_N2, body, 0, unroll=FFT_UNROLL)


def _fft_inner_block(bre_ref, bim_ref, mf_ref, f1):
    xr = bre_ref[pl.ds(f1, FFT_N2, stride=FFT_N1), :]
    xi = bim_ref[pl.ds(f1, FFT_N2, stride=FFT_N1), :]
    xx = jnp.concatenate([xr, xi], axis=0).astype(BF16)
    z = _dot(mf_ref[...], xx)
    return z[:FFT_N2], z[FFT_N2:]


def _filter_fft_kernel(feats_ref, w1_ref, b1_ref, w2_ref, b2_ref, w3f_ref, w3b_ref, freq_ref, ld_ref,
                       m1_ref, mf_ref, spec_ref, k_ref, bre_ref, bim_ref, hid_ref):
    kf, kb = _filter_taps(feats_ref, w1_ref, b1_ref, w2_ref, b2_ref, w3f_ref, w3b_ref, freq_ref, ld_ref, hid_ref)
    n = kf.shape[0]
    k_ref[pl.ds(0, n), :] = kf
    k_ref[pl.ds(n, n), :] = kb
    _fft_outer_forward(k_ref, m1_ref, bre_ref, bim_ref, FFT_N1)

    def body(f1, carry):
        zr, zi = _fft_inner_block(bre_ref, bim_ref, mf_ref, f1)
        base = pl.multiple_of(f1 * FFT_N2, FFT_N2)
        spec_ref[0, 0, pl.ds(base, FFT_N2), :] = zr.astype(BF16)
        spec_ref[0, 1, pl.ds(base, FFT_N2), :] = zi.astype(BF16)
        return carry
    lax.fori_loop(0, FFT_N1, body, 0, unroll=FFT_UNROLL // 2)


def _hyena_spectrum(n, feats, w1, b1, w2, b2, w3, freq, ld, tables):
    nn = 2 * n
    nchunk = HY_WIDTH // LANES
    c2 = lambda o, c: (0, 0)
    one = pl.Buffered(1)
    in_specs = [pl.BlockSpec((nn, LANES), c2, pipeline_mode=one),
                pl.BlockSpec((LANES, LANES), c2), pl.BlockSpec((1, LANES), c2),
                pl.BlockSpec((LANES, LANES), c2), pl.BlockSpec((1, LANES), c2),
                pl.BlockSpec((1, LANES, LANES), lambda o, c: (o * 2 * nchunk + c, 0, 0)),
                pl.BlockSpec((1, LANES, LANES), lambda o, c: (o * 2 * nchunk + nchunk + c, 0, 0)),
                pl.BlockSpec((2, LANES), c2),
                pl.BlockSpec((1, 2, LANES), lambda o, c: (o * nchunk + c, 0, 0))]
    args = [feats, w1, b1, w2, b2, w3, w3, freq, ld]
    out_spec = pl.BlockSpec((1, 2, nn, LANES), lambda o, c: (o, 0, 0, c))
    out_shape = jax.ShapeDtypeStruct((HY_ORDER, 2, nn, HY_WIDTH), BF16)
    if "md" in tables:
        return pl.pallas_call(
            _filter_dense_kernel, grid=(HY_ORDER, nchunk),
            in_specs=in_specs + [pl.BlockSpec((2 * nn, nn), c2)],
            out_specs=out_spec, out_shape=out_shape,
            scratch_shapes=[pltpu.VMEM((nn, LANES), F32)],
            compiler_params=_cparams("arbitrary", "arbitrary"), name="hyena_filter_dense",
        )(*args, tables["md"])
    return pl.pallas_call(
        _filter_fft_kernel, grid=(HY_ORDER, nchunk),
        in_specs=in_specs + [pl.BlockSpec((FFT_N2, 2 * FFT_N1, FFT_N1), lambda o, c: (0, 0, 0), pipeline_mode=one),
                             pl.BlockSpec((2 * FFT_N2, 2 * FFT_N2), c2)],
        out_specs=out_spec, out_shape=out_shape,
        scratch_shapes=[pltpu.VMEM((nn, LANES), F32)] * 4,
        compiler_params=_cparams("arbitrary", "arbitrary"), name="hyena_filter_fft",
    )(*args, tables["m1_full"], tables["mf"])


def _short_conv(u, cw, cb):
    n = u.shape[0]
    row = lax.broadcasted_iota(jnp.int32, u.shape, 0)
    prev = jnp.where(row == 0, 0.0, pltpu.roll(u, 1, 0))
    nxt = jnp.where(row == n - 1, 0.0, pltpu.roll(u, n - 1, 0))
    return prev * cw[0:1, :] + u * cw[1:2, :] + nxt * cw[2:3, :] + cb


def _hyena_dense_kernel(x1_ref, x2_ref, v_ref, cw1_ref, cw2_ref, cwv_ref, cb1_ref, cb2_ref, cbv_ref, skip_ref,
                        spec_ref, md_ref, mdi_ref, o_ref):
    n = v_ref.shape[1]
    nn = 2 * n
    z = _short_conv(v_ref[0].astype(F32), cwv_ref[...], cbv_ref[...])
    gates = ((x1_ref, cw1_ref, cb1_ref), (x2_ref, cw2_ref, cb2_ref))
    for o, (g_ref, cw_ref, cb_ref) in enumerate(gates):
        zf = _dot(md_ref[...], z.astype(BF16))
        zr, zi = zf[:nn], zf[nn:]
        sr = spec_ref[o, 0].astype(F32)
        si = spec_ref[o, 1].astype(F32)
        yy = jnp.concatenate([zr * sr - zi * si, zr * si + zi * sr], axis=0).astype(BF16)
        y = _dot(mdi_ref[...], yy) * (1.0 / nn)
        gate = _short_conv(g_ref[0].astype(F32), cw_ref[...], cb_ref[...])
        z = gate * (y + skip_ref[o:o + 1, :] * z)
    o_ref[0] = z.astype(BF16)


def _hyena_fft_kernel(x1_ref, x2_ref, v_ref, cw1_ref, cw2_ref, cwv_ref, cb1_ref, cb2_ref, cbv_ref, skip_ref,
                      spec_ref, m1_ref, mf_ref, mi_ref, m3_ref, o_ref,
                      z_ref, bre_ref, bim_ref, pre_ref, pim_ref):
    n = v_ref.shape[1]
    nn = 2 * n
    k1 = FFT_N1 // 2
    z_ref[...] = _short_conv(v_ref[0].astype(F32), cwv_ref[...], cbv_ref[...])
    gates = ((x1_ref, cw1_ref, cb1_ref), (x2_ref, cw2_ref, cb2_ref))
    for o, (g_ref, cw_ref, cb_ref) in enumerate(gates):
        _fft_outer_forward(z_ref, m1_ref, bre_ref, bim_ref, k1)

        def mid(f1, carry):
            zr, zi = _fft_inner_block(bre_ref, bim_ref, mf_ref, f1)
            base = pl.multiple_of(f1 * FFT_N2, FFT_N2)
            sr = spec_ref[o, 0, pl.ds(base, FFT_N2), :].astype(F32)
            si = spec_ref[o, 1, pl.ds(base, FFT_N2), :].astype(F32)
            yy = jnp.concatenate([zr * sr - zi * si, zr * si + zi * sr], axis=0).astype(BF16)
            p = _dot(mi_ref[...], yy)
            pre_ref[pl.ds(base, FFT_N2), :] = p[:FFT_N2]
            pim_ref[pl.ds(base, FFT_N2), :] = p[FFT_N2:]
            return carry
        lax.fori_loop(0, FFT_N1, mid, 0, unroll=FFT_UNROLL)

        def last(t2, carry):
            pr = pre_ref[pl.ds(t2, FFT_N1, stride=FFT_N2), :]
            pi_ = pim_ref[pl.ds(t2, FFT_N1, stride=FFT_N2), :]
            xx = jnp.concatenate([pr, pi_], axis=0).astype(BF16)
            yv = _dot(m3_ref[t2], xx) * (1.0 / nn)
            bre_ref[pl.ds(t2, k1, stride=FFT_N2), :] = yv
            return carry
        lax.fori_loop(0, FFT_N2, last, 0, unroll=FFT_UNROLL)

        gate = _short_conv(g_ref[0].astype(F32), cw_ref[...], cb_ref[...])
        z_ref[...] = gate * (bre_ref[pl.ds(0, n), :] + skip_ref[o:o + 1, :] * z_ref[...])
    o_ref[0] = z_ref[...].astype(BF16)


def _hyena_conv(u_hy, conv_w, conv_b, skip, spec, tables):
    b, n, _ = u_hy.shape
    nn = 2 * n
    nchunk = HY_WIDTH // LANES
    cb = conv_b.reshape(1, 3 * HY_WIDTH)

    def part(p):
        return pl.BlockSpec((1, n, LANES), lambda c, bi: (bi, 0, p * nchunk + c))

    def cwp(p):
        return pl.BlockSpec((3, LANES), lambda c, bi: (0, p * nchunk + c))

    def cbp(p):
        return pl.BlockSpec((1, LANES), lambda c, bi: (0, p * nchunk + c))

    in_specs = [part(0), part(1), part(2), cwp(0), cwp(1), cwp(2), cbp(0), cbp(1), cbp(2),
                pl.BlockSpec((HY_ORDER, LANES), lambda c, bi: (0, c)),
                pl.BlockSpec((HY_ORDER, 2, nn, LANES), lambda c, bi: (0, 0, 0, c), pipeline_mode=pl.Buffered(1))]
    args = [u_hy, u_hy, u_hy, conv_w, conv_w, conv_w, cb, cb, cb, skip, spec]
    out_spec = pl.BlockSpec((1, n, LANES), lambda c, bi: (bi, 0, c))
    out_shape = jax.ShapeDtypeStruct((b, n, HY_WIDTH), BF16)
    c2 = lambda c, bi: (0, 0)
    c3 = lambda c, bi: (0, 0, 0)
    if "md" in tables:
        return pl.pallas_call(
            _hyena_dense_kernel, grid=(nchunk, b),
            in_specs=in_specs + [pl.BlockSpec((2 * nn, n), c2), pl.BlockSpec((n, 2 * nn), c2)],
            out_specs=out_spec, out_shape=out_shape,
            compiler_params=_cparams("parallel", "parallel"), name="hyena_conv_dense",
        )(*args, tables["md_half"], tables["mdi"])
    one = pl.Buffered(1)
    return pl.pallas_call(
        _hyena_fft_kernel, grid=(nchunk, b),
        in_specs=in_specs + [pl.BlockSpec((FFT_N2, 2 * FFT_N1, FFT_N1 // 2), c3, pipeline_mode=one),
                             pl.BlockSpec((2 * FFT_N2, 2 * FFT_N2), c2),
                             pl.BlockSpec((2 * FFT_N2, 2 * FFT_N2), c2),
                             pl.BlockSpec((FFT_N2, FFT_N1 // 2, 2 * FFT_N1), c3, pipeline_mode=one)],
        out_specs=out_spec, out_shape=out_shape,
        scratch_shapes=[pltpu.VMEM((n, LANES), F32)] + [pltpu.VMEM((nn, LANES), F32)] * 4,
        compiler_params=_cparams("parallel", "parallel"), name="hyena_conv_fft",
    )(*args, tables["m1_half"], tables["mf"], tables["mi"], tables["m3"])


def _fnet_kernel(u_ref, bc_ref, bs_ref, lhs_ref, o_ref, xcs_ref):
    n = u_ref.shape[1]

    @pl.when(pl.program_id(1) == 0)
    def _():
        u = u_ref[0]
        xcs_ref[pl.ds(0, n), :] = _dot(u, bc_ref[...]).astype(BF16)
        xcs_ref[pl.ds(n, n), :] = _dot(u, bs_ref[...]).astype(BF16)

    scale = 1.0 / math.sqrt(n * FN_GROUP_DIM)
    o_ref[0] = (_dot(lhs_ref[...], xcs_ref[...]) * scale).astype(BF16)


def _fnet_fft_kernel(u_ref, bc_ref, bs_ref, m1_ref, m2_ref, o_ref, wre_ref, wim_ref, bre_ref, bim_ref):
    n = u_ref.shape[1]
    r = FN_RADIX
    u = u_ref[0]
    wre_ref[...] = _dot(u, bc_ref[...])
    wim_ref[...] = -_dot(u, bs_ref[...])

    def outer(s2, carry):
        xx = jnp.concatenate([wre_ref[pl.ds(s2, r, stride=r), :], wim_ref[pl.ds(s2, r, stride=r), :]], axis=0)
        z = _dot(m1_ref[s2], xx.astype(BF16))
        base = pl.multiple_of(s2 * r, r)
        bre_ref[pl.ds(base, r), :] = z[:r]
        bim_ref[pl.ds(base, r), :] = z[r:]
        return carry
    lax.fori_loop(0, r, outer, 0, unroll=FFT_UNROLL)

    scale = 1.0 / math.sqrt(n * FN_GROUP_DIM)

    def inner(f1, carry):
        xx = jnp.concatenate([bre_ref[pl.ds(f1, r, stride=r), :], bim_ref[pl.ds(f1, r, stride=r), :]], axis=0)
        wre_ref[pl.ds(f1, r, stride=r), :] = _dot(m2_ref[...], xx.astype(BF16)) * scale
        return carry
    lax.fori_loop(0, r, inner, 0, unroll=FFT_UNROLL)
    o_ref[0] = wre_ref[...].astype(BF16)


def _fnet_fft(u_fn, tables):
    b, n, _ = u_fn.shape
    r = FN_RADIX
    nchunk = FN_WIDTH // LANES
    return pl.pallas_call(
        _fnet_fft_kernel, grid=(b, nchunk),
        in_specs=[pl.BlockSpec((1, n, FN_WIDTH), lambda bi, c: (bi, 0, 0)),
                  pl.BlockSpec((FN_WIDTH, LANES), lambda bi, c: (0, c)),
                  pl.BlockSpec((FN_WIDTH, LANES), lambda bi, c: (0, c)),
                  pl.BlockSpec((r, 2 * r, 2 * r), lambda bi, c: (0, 0, 0)),
                  pl.BlockSpec((r, 2 * r), lambda bi, c: (0, 0))],
        out_specs=pl.BlockSpec((1, n, LANES), lambda bi, c: (bi, 0, c)),
        out_shape=jax.ShapeDtypeStruct((b, n, FN_WIDTH), BF16),
        scratch_shapes=[pltpu.VMEM((n, LANES), F32)] * 4,
        compiler_params=_cparams("parallel", "parallel"), name="fnet_fft",
    )(u_fn, tables["bc"], tables["bs"], tables["fn_m1"], tables["fn_m2"])


def _fnet(u_fn, tables):
    if "fn_m1" in tables:
        return _fnet_fft(u_fn, tables)
    b, n, _ = u_fn.shape
    tq = min(FNET_TILE, n)
    return pl.pallas_call(
        _fnet_kernel, grid=(b, n // tq),
        in_specs=[pl.BlockSpec((1, n, FN_WIDTH), lambda bi, i: (bi, 0, 0)),
                  pl.BlockSpec((FN_WIDTH, FN_WIDTH), lambda bi, i: (0, 0)),
                  pl.BlockSpec((FN_WIDTH, FN_WIDTH), lambda bi, i: (0, 0)),
                  pl.BlockSpec((tq, 2 * n), lambda bi, i: (i, 0))],
        out_specs=pl.BlockSpec((1, tq, FN_WIDTH), lambda bi, i: (bi, i, 0)),
        out_shape=jax.ShapeDtypeStruct((b, n, FN_WIDTH), BF16),
        scratch_shapes=[pltpu.VMEM((2 * n, FN_WIDTH), BF16)],
        compiler_params=_cparams("parallel", "arbitrary"), name="fnet",
    )(u_fn, tables["bc"], tables["bs"], tables["fn_lhs"])


def _kv_kernel(ckv_ref, ks_ref, cosk_ref, wk_ref, wv_ref, k_ref, v_ref):
    ckv = ckv_ref[0].astype(BF16)
    kr = (ks_ref[0] * cosk_ref[...]).astype(BF16)
    k_ref[0] = _dot(jnp.concatenate([ckv, kr], axis=-1), wk_ref[...]).astype(BF16)
    v = _dot(ckv, wv_ref[...])
    lane = lax.broadcasted_iota(jnp.int32, v.shape, 1)
    v_ref[0] = jnp.where(lane % HEAD_PAD == V_HEAD_DIM, 1.0, v).astype(BF16)


def _kv_up(ckv_all, ks_all, cosk, wk_aug, wv):
    b, lk, _ = ckv_all.shape
    tk = ROW_TILE
    hw = N_HEADS * HEAD_PAD
    vw = N_HEADS * HEAD_PAD
    return pl.pallas_call(
        _kv_kernel, grid=(b, lk // tk),
        in_specs=[pl.BlockSpec((1, tk, KV_LORA_RANK), lambda bi, i: (bi, i, 0)),
                  pl.BlockSpec((1, tk, LANES), lambda bi, i: (bi, i, 0)),
                  pl.BlockSpec((tk, LANES), lambda bi, i: (i, 0)),
                  pl.BlockSpec((KV_LORA_RANK + LANES, hw), lambda bi, i: (0, 0)),
                  pl.BlockSpec((KV_LORA_RANK, vw), lambda bi, i: (0, 0))],
        out_specs=[pl.BlockSpec((1, tk, hw), lambda bi, i: (bi, i, 0)),
                   pl.BlockSpec((1, tk, vw), lambda bi, i: (bi, i, 0))],
        out_shape=[jax.ShapeDtypeStruct((b, lk, hw), BF16), jax.ShapeDtypeStruct((b, lk, vw), BF16)],
        compiler_params=_cparams("parallel", "parallel"), name="kv_up",
    )(ckv_all, ks_all, cosk, wk_aug, wv)


def _attn_kernel(q_ref, k_ref, v_ref, o_ref):
    outs = []
    for h in range(N_HEADS):
        q = q_ref[0, :, h * HEAD_PAD:(h + 1) * HEAD_PAD]
        k = k_ref[0, :, h * HEAD_PAD:(h + 1) * HEAD_PAD]
        s = lax.dot_general(q, k, (((1,), (1,)), ((), ())), preferred_element_type=F32)
        m = jnp.max(s, axis=-1, keepdims=True)
        p = jnp.exp2((s - m).astype(BF16))
        ov = _dot(p, v_ref[0, :, h * HEAD_PAD:(h + 1) * HEAD_PAD])
        outs.append(ov[:, :V_HEAD_DIM] * (1.0 / ov[:, V_HEAD_DIM:V_HEAD_DIM + 1]))
    o_ref[0] = jnp.concatenate(outs, axis=-1).astype(BF16)


def _attention(q, k, v):
    b, n, hw = q.shape
    lk = k.shape[1]
    vw = N_HEADS * V_HEAD_DIM
    tq = ATTN_Q_TILE
    one = pl.Buffered(1)
    return pl.pallas_call(
        _attn_kernel, grid=(b, n // tq),
        in_specs=[pl.BlockSpec((1, tq, hw), lambda bi, i: (bi, i, 0)),
                  pl.BlockSpec((1, lk, hw), lambda bi, i: (bi, 0, 0), pipeline_mode=one),
                  pl.BlockSpec((1, lk, hw), lambda bi, i: (bi, 0, 0), pipeline_mode=one)],
        out_specs=pl.BlockSpec((1, tq, vw), lambda bi, i: (bi, i, 0)),
        out_shape=jax.ShapeDtypeStruct((b, n, vw), BF16),
        compiler_params=_cparams("parallel", "parallel"), name="attention",
    )(q, k, v)


def _out_proj_kernel(hy_ref, fn_ref, att_ref, x_ref, mod_ref, wo_ref, gpost_ref, gffn_ref, rw_ref, rb_ref,
                     x1_ref, h2_ref, lg_ref):
    o1 = HY_WIDTH
    o2 = o1 + FN_WIDTH
    m = (_dot(hy_ref[0], wo_ref[:o1, :]) + _dot(fn_ref[0], wo_ref[o1:o2, :]) + _dot(att_ref[0], wo_ref[o2:, :]))
    g1 = mod_ref[0, 2:3, :]
    sh2 = mod_ref[0, 3:4, :]
    sc2 = mod_ref[0, 4:5, :]
    x1 = x_ref[0] + g1 * _rms(m, gpost_ref[...])
    x1_ref[0] = x1
    h2 = _rms(x1, gffn_ref[...]) * (1.0 + sc2) + sh2
    h2_ref[0] = h2
    lg_ref[0] = _dot_hi(h2, rw_ref[...]) + rb_ref[...]


def _out_proj(hy, fn, att, x, mod, w_out, g_post, g_ffn, rw_pad, rb_pad):
    b, n, _ = x.shape
    tm = ROW_TILE
    mix = HY_WIDTH + FN_WIDTH + N_HEADS * V_HEAD_DIM
    mod_map = (lambda bi, i: (bi, 0, 0)) if mod.shape[0] > 1 else (lambda bi, i: (0, 0, 0))
    const = lambda bi, i: (0, 0)
    row = lambda w: pl.BlockSpec((1, tm, w), lambda bi, i: (bi, i, 0))
    return pl.pallas_call(
        _out_proj_kernel, grid=(b, n // tm),
        in_specs=[row(HY_WIDTH), row(FN_WIDTH), row(N_HEADS * V_HEAD_DIM), row(D_MODEL),
                  pl.BlockSpec((1, 6, D_MODEL), mod_map),
                  pl.BlockSpec((mix, D_MODEL), const),
                  pl.BlockSpec((1, D_MODEL), const), pl.BlockSpec((1, D_MODEL), const),
                  pl.BlockSpec((D_MODEL, LANES), const), pl.BlockSpec((1, LANES), const)],
        out_specs=[row(D_MODEL), row(D_MODEL), row(LANES)],
        out_shape=[jax.ShapeDtypeStruct((b, n, D_MODEL), F32),
                   jax.ShapeDtypeStruct((b, n, D_MODEL), F32),
                   jax.ShapeDtypeStruct((b, n, LANES), F32)],
        compiler_params=_cparams("parallel", "parallel"), name="out_proj",
    )(hy, fn, att, x, mod, w_out, g_post, g_ffn, rw_pad, rb_pad)


def _route_kernel(lg_ref, info_ref, cnt_ref, carry_ref):
    i = pl.program_id(0)

    @pl.when(i == 0)
    def _():
        carry_ref[...] = jnp.zeros_like(carry_ref)

    v = lg_ref[...]
    tm = v.shape[0]
    lane = lax.broadcasted_iota(jnp.int32, v.shape, 1)
    sels, vals, idxs = [], [], []
    for _ in range(TOP_K):
        m = jnp.max(v, axis=-1, keepdims=True)
        idx = jnp.min(jnp.where(v == m, lane, LANES), axis=-1, keepdims=True)
        sel = lane == idx
        sels.append(sel)
        vals.append(m)
        idxs.append(idx)
        v = jnp.where(sel, -jnp.inf, v)
    es = [jnp.exp(val - vals[0]) for val in vals]
    den = es[0] + es[1] + es[2] + es[3]
    onehot = jnp.where(sels[0] | sels[1] | sels[2] | sels[3], 1.0, 0.0)
    r_i = lax.broadcasted_iota(jnp.int32, (tm, tm), 0)
    c_i = lax.broadcasted_iota(jnp.int32, (tm, tm), 1)
    tri = jnp.where(c_i < r_i, 1.0, 0.0).astype(BF16)
    rank = _dot(tri, onehot.astype(BF16)) + carry_ref[0:1, :]
    info = jnp.zeros(v.shape, F32)
    for kk in range(TOP_K):
        rk = jnp.sum(jnp.where(sels[kk], rank, 0.0), axis=-1, keepdims=True)
        info = jnp.where(lane == kk, es[kk] / den, info)
        info = jnp.where(lane == TOP_K + kk, idxs[kk].astype(F32), info)
        info = jnp.where(lane == 2 * TOP_K + kk, rk, info)
    info_ref[...] = info
    carry_ref[0:1, :] = carry_ref[0:1, :] + jnp.sum(onehot, axis=0, keepdims=True)
    cnt_ref[...] = carry_ref[...]


def _route(logits):
    t = logits.shape[0]
    tm = ROUTE_TILE
    return pl.pallas_call(
        _route_kernel, grid=(t // tm,),
        in_specs=[pl.BlockSpec((tm, LANES), lambda i: (i, 0))],
        out_specs=[pl.BlockSpec((tm, LANES), lambda i: (i, 0)), pl.BlockSpec((8, LANES), lambda i: (0, 0))],
        out_shape=[jax.ShapeDtypeStruct((t, LANES), F32), jax.ShapeDtypeStruct((8, LANES), F32)],
        scratch_shapes=[pltpu.VMEM((8, LANES), F32)],
        compiler_params=_cparams("arbitrary"), name="route",
    )(logits)


def _dispatch_kernel(dest_ref, h_ref, xb_in_ref, xb_ref, idx_ref, isem, sem):
    del xb_in_ref
    i = pl.program_id(0)
    rows = h_ref.shape[0]
    cp = pltpu.make_async_copy(dest_ref.at[i], idx_ref, isem)
    cp.start()
    cp.wait()

    def row_copy(r, kk, slot):
        return pltpu.make_async_copy(h_ref.at[pl.ds(r, 1)], xb_ref.at[pl.ds(slot, 1)], sem)

    def issue(r, carry):
        for kk in range(TOP_K):
            row_copy(r, kk, idx_ref[r * TOP_K + kk]).start()
        return carry
    lax.fori_loop(0, rows, issue, 0, unroll=DMA_UNROLL)

    def drain(r, carry):
        for kk in range(TOP_K):
            row_copy(r, kk, 0).wait()
        return carry
    lax.fori_loop(0, rows, drain, 0, unroll=DMA_UNROLL)


def _dispatch(h2, dest, n_slots):
    t = h2.shape[0]
    rows = DMA_ROWS
    xb0 = jnp.zeros((n_slots, D_MODEL), F32)
    return pl.pallas_call(
        _dispatch_kernel, grid=(t // rows,),
        in_specs=[pl.BlockSpec(memory_space=pl.ANY),
                  pl.BlockSpec((rows, D_MODEL), lambda i: (i, 0)),
                  pl.BlockSpec(memory_space=pl.ANY)],
        out_specs=pl.BlockSpec(memory_space=pl.ANY),
        out_shape=jax.ShapeDtypeStruct((n_slots, D_MODEL), F32),
        scratch_shapes=[pltpu.SMEM((rows * TOP_K,), jnp.int32),
                        pltpu.SemaphoreType.DMA(()), pltpu.SemaphoreType.DMA(())],
        input_output_aliases={2: 0},
        compiler_params=_cparams("arbitrary"), name="moe_dispatch",
    )(dest.reshape(t // rows, rows * TOP_K), h2, xb0)


def _deinterleave_kernel(w_ref, s_ref, o_ref):
    for k in range(w_ref.shape[2] // MXU_TILE):
        blk = w_ref[0, :, k * MXU_TILE:(k + 1) * MXU_TILE].astype(BF16)
        o_ref[0, :, k * MXU_TILE:(k + 1) * MXU_TILE] = _dot(blk, s_ref[...]).astype(BF16)


def _deinterleave_gate_up(w_gu):
    ne, d, f2 = w_gu.shape
    tr = 512
    half = MXU_TILE // 2
    sel = np.zeros((MXU_TILE, MXU_TILE), np.float32)
    for j in range(half):
        sel[2 * j, j] = 1.0
        sel[2 * j + 1, half + j] = 1.0
    return pl.pallas_call(
        _deinterleave_kernel, grid=(ne, d // tr),
        in_specs=[pl.BlockSpec((1, tr, f2), lambda e, i: (e, i, 0)),
                  pl.BlockSpec((MXU_TILE, MXU_TILE), lambda e, i: (0, 0))],
        out_specs=pl.BlockSpec((1, tr, f2), lambda e, i: (e, i, 0)),
        out_shape=jax.ShapeDtypeStruct((ne, d, f2), BF16),
        compiler_params=_cparams("parallel", "parallel"), name="moe_weight_prep",
    )(w_gu, jnp.asarray(sel, BF16))


def _expert_kernel(be_ref, nu_ref, x_ref, wgu_ref, bgu_ref, wd_ref, bd_ref, o_ref):
    i = pl.program_id(0)

    @pl.when(i < nu_ref[0])
    def _():
        x = x_ref[...].astype(BF16)
        gu = _dot(x, wgu_ref[0]) + bgu_ref[0]
        half = MXU_TILE // 2
        acts = []
        for k in range(gu.shape[1] // MXU_TILE):
            g = jnp.minimum(gu[:, k * MXU_TILE:k * MXU_TILE + half], SWIGLU_LIMIT)
            lin = jnp.clip(gu[:, k * MXU_TILE + half:(k + 1) * MXU_TILE], -SWIGLU_LIMIT, SWIGLU_LIMIT)
            acts.append((g * jax.nn.sigmoid(SWIGLU_ALPHA * g) * (lin + 1.0)).astype(BF16))
        o_ref[...] = _dot(jnp.concatenate(acts, axis=-1), wd_ref[0]) + bd_ref[0]

    @pl.when(i >= nu_ref[0])
    def _():
        o_ref[...] = jnp.zeros_like(o_ref)


def _experts(xb, block_exp, n_used, wgu, bgu, wd, bd):
    n_slots = xb.shape[0]
    bm = MOE_BLOCK
    wmap = lambda i, be, nu: (be[i], 0, 0)
    return pl.pallas_call(
        _expert_kernel,
        grid_spec=pltpu.PrefetchScalarGridSpec(
            num_scalar_prefetch=2, grid=(n_slots // bm,),
            in_specs=[pl.BlockSpec((bm, D_MODEL), lambda i, be, nu: (i, 0)),
                      pl.BlockSpec((1, D_MODEL, 2 * D_FF), wmap), pl.BlockSpec((1, 1, 2 * D_FF), wmap),
                      pl.BlockSpec((1, D_FF, D_MODEL), wmap), pl.BlockSpec((1, 1, D_MODEL), wmap)],
            out_specs=pl.BlockSpec((bm, D_MODEL), lambda i, be, nu: (i, 0))),
        out_shape=jax.ShapeDtypeStruct((n_slots, D_MODEL), F32),
        compiler_params=_cparams("arbitrary"), name="moe_experts",
    )(block_exp, n_used, xb, wgu, bgu, wd, bd)


def _combine_kernel(dest_ref, ys_ref, info_ref, x1_ref, mod_ref, gpost_ref, o_ref, idx_ref, buf_ref, isem, sem):
    i = pl.program_id(1)
    nb = pl.num_programs(1)
    rows = x1_ref.shape[1]
    cp = pltpu.make_async_copy(dest_ref.at[pl.program_id(0) * nb + i], idx_ref, isem)
    cp.start()
    cp.wait()

    def row_copy(r, kk, slot):
        return pltpu.make_async_copy(ys_ref.at[pl.ds(slot, 1)], buf_ref.at[kk, pl.ds(r, 1)], sem)

    def issue(r, carry):
        for kk in range(TOP_K):
            row_copy(r, kk, idx_ref[r * TOP_K + kk]).start()
        return carry
    lax.fori_loop(0, rows, issue, 0, unroll=DMA_UNROLL)

    def drain(r, carry):
        for kk in range(TOP_K):
            row_copy(r, kk, 0).wait()
        return carry
    lax.fori_loop(0, rows, drain, 0, unroll=DMA_UNROLL)

    info = info_ref[...]
    f = info[:, 0:1] * buf_ref[0]
    for kk in range(1, TOP_K):
        f = f + info[:, kk:kk + 1] * buf_ref[kk]
    g2 = mod_ref[0, 5:6, :]
    o_ref[0] = x1_ref[0] + g2 * _rms(f, gpost_ref[...])


def _combine(ys, dest, info, x1, mod, g_post):
    b, n, _ = x1.shape
    rows = min(DMA_ROWS, n)
    nb = n // rows
    mod_map = (lambda bi, i: (bi, 0, 0)) if mod.shape[0] > 1 else (lambda bi, i: (0, 0, 0))
    return pl.pallas_call(
        _combine_kernel, grid=(b, nb),
        in_specs=[pl.BlockSpec(memory_space=pl.ANY),
                  pl.BlockSpec(memory_space=pl.ANY),
                  pl.BlockSpec((rows, LANES), lambda bi, i: (bi * nb + i, 0)),
                  pl.BlockSpec((1, rows, D_MODEL), lambda bi, i: (bi, i, 0)),
                  pl.BlockSpec((1, 6, D_MODEL), mod_map),
                  pl.BlockSpec((1, D_MODEL), lambda bi, i: (0, 0))],
        out_specs=pl.BlockSpec((1, rows, D_MODEL), lambda bi, i: (bi, i, 0)),
        out_shape=jax.ShapeDtypeStruct((b, n, D_MODEL), F32),
        scratch_shapes=[pltpu.SMEM((rows * TOP_K,), jnp.int32),
                        pltpu.VMEM((TOP_K, rows, D_MODEL), F32),
                        pltpu.SemaphoreType.DMA(()), pltpu.SemaphoreType.DMA(())],
        compiler_params=_cparams("arbitrary", "arbitrary"), name="moe_combine",
    )(dest.reshape(b * nb, rows * TOP_K), ys, info, x1, mod, g_post)


def _moe_and_residual(h2, logits, x1, mod, g_post, ew):
    b, n, _ = x1.shape
    t = b * n
    info, cnt = _route(logits.reshape(t, LANES))
    counts = cnt[0, :N_EXPERTS].astype(jnp.int32)
    bm = MOE_BLOCK
    padded = (counts + bm - 1) // bm * bm
    pad_end = jnp.cumsum(padded)
    pad_start = pad_end - padded
    e_idx = info[:, TOP_K:2 * TOP_K].astype(jnp.int32)
    rank = info[:, 2 * TOP_K:3 * TOP_K].astype(jnp.int32)
    ids = jnp.arange(N_EXPERTS, dtype=jnp.int32)
    start_of = jnp.sum(jnp.where(e_idx[:, :, None] == ids, pad_start, 0), axis=-1)
    dest = (start_of + rank).reshape(-1)
    n_blocks = t * TOP_K // bm + N_EXPERTS
    n_slots = n_blocks * bm
    first_row = jnp.arange(n_blocks, dtype=jnp.int32) * bm
    block_exp = jnp.minimum(jnp.sum((pad_end[None, :] <= first_row[:, None]).astype(jnp.int32), axis=-1),
                            N_EXPERTS - 1)
    n_used = (pad_end[-1:] // bm).astype(jnp.int32)
    xb = _dispatch(h2.reshape(t, D_MODEL), dest, n_slots)
    ys = _experts(xb, block_exp, n_used, *ew)
    return _combine(ys, dest, info, x1, mod, g_post)


def _cos_sin(num, den, shape_like=None):
    ang = (2.0 * math.pi / den) * num.astype(F32)
    return jnp.cos(ang), jnp.sin(ang)


def _dense_dft_tables(n):
    nn = 2 * n
    f = jnp.arange(nn, dtype=jnp.int32)[:, None]
    s = jnp.arange(nn, dtype=jnp.int32)[None, :]
    c, sn = _cos_sin((f * s) % nn, nn)
    md = jnp.concatenate([c, -sn], axis=0).astype(BF16)
    ci, si = c[:n, :], sn[:n, :]
    mdi = jnp.concatenate([ci, -si], axis=1).astype(BF16)
    return {"md": md, "md_half": md[:, :n], "mdi": mdi}


def _fft_tables(n):
    nn = 2 * n
    n1, n2 = FFT_N1, FFT_N2
    assert nn == n1 * n2
    s2 = jnp.arange(n2, dtype=jnp.int32)[:, None, None]
    f1 = jnp.arange(n1, dtype=jnp.int32)[None, :, None]
    s1 = jnp.arange(n1, dtype=jnp.int32)[None, None, :]
    c, sn = _cos_sin((f1 * (n2 * s1 + s2)) % nn, nn)
    m1 = jnp.concatenate([c, -sn], axis=1).astype(BF16)
    a = jnp.arange(n2, dtype=jnp.int32)
    c2, sn2 = _cos_sin((a[:, None] * a[None, :]) % n2, n2)
    mf = jnp.concatenate([jnp.concatenate([c2, sn2], axis=1),
                          jnp.concatenate([-sn2, c2], axis=1)], axis=0).astype(BF16)
    mi = jnp.concatenate([jnp.concatenate([c2, -sn2], axis=1),
                          jnp.concatenate([sn2, c2], axis=1)], axis=0).astype(BF16)
    t2 = jnp.arange(n2, dtype=jnp.int32)[:, None, None]
    t1 = jnp.arange(n1 // 2, dtype=jnp.int32)[None, :, None]
    g1 = jnp.arange(n1, dtype=jnp.int32)[None, None, :]
    c3, sn3 = _cos_sin((g1 * (n2 * t1 + t2)) % nn, nn)
    m3 = jnp.concatenate([c3, -sn3], axis=2).astype(BF16)
    return {"m1_full": m1, "m1_half": m1[:, :, :n1 // 2], "mf": mf, "mi": mi, "m3": m3}


def _fnet_tables(n):
    j = jnp.arange(FN_WIDTH, dtype=jnp.int32)
    same = (j[:, None] // FN_GROUP_DIM) == (j[None, :] // FN_GROUP_DIM)
    c, s = _cos_sin(((j[:, None] % FN_GROUP_DIM) * (j[None, :] % FN_GROUP_DIM)) % FN_GROUP_DIM, FN_GROUP_DIM)
    bc = jnp.where(same, c, 0.0).astype(BF16)
    bs = jnp.where(same, s, 0.0).astype(BF16)
    r = FN_RADIX
    if n == r * r:
        s2 = jnp.arange(r, dtype=jnp.int32)[:, None, None]
        f1 = jnp.arange(r, dtype=jnp.int32)[None, :, None]
        s1 = jnp.arange(r, dtype=jnp.int32)[None, None, :]
        c1, sn1 = _cos_sin((f1 * (r * s1 + s2)) % n, n)
        m1 = jnp.concatenate([jnp.concatenate([c1, sn1], axis=2),
                              jnp.concatenate([-sn1, c1], axis=2)], axis=1).astype(BF16)
        a = jnp.arange(r, dtype=jnp.int32)
        c2, sn2 = _cos_sin((a[:, None] * a[None, :]) % r, r)
        m2 = jnp.concatenate([c2, sn2], axis=1).astype(BF16)
        return {"bc": bc, "bs": bs, "fn_m1": m1, "fn_m2": m2}
    p = jnp.arange(n, dtype=jnp.int32)
    cl, sl = _cos_sin((p[:, None] * p[None, :]) % n, n)
    return {"bc": bc, "bs": bs, "fn_lhs": jnp.concatenate([cl, -sl], axis=1).astype(BF16)}


def _filter_feats(n):
    r = jnp.arange(2 * n, dtype=jnp.int32)
    pos = jnp.where(r < n, r, 2 * n - r).astype(F32)
    t = pos / n
    bands = jnp.linspace(1e-4, FILT_BANDS - 1, FILT_BANDS, dtype=F32)
    ang = (2.0 * math.pi / n) * pos[:, None] * bands[None, :]
    feats = jnp.concatenate([t[:, None], jnp.cos(ang), -jnp.sin(ang)], axis=-1)
    return jnp.pad(feats, ((0, 0), (0, LANES - FILT_EMB)))


def _rope_tables(n, use_rope):
    if use_rope:
        tpos = jnp.arange(n, dtype=jnp.int32)
        rows = (tpos // GRID_W).astype(F32)
        cols = (tpos % GRID_W).astype(F32)
        inv = ROPE_BASE ** (-jnp.arange(AXIS_PAIRS, dtype=F32) / AXIS_PAIRS)
        ang = jnp.stack([rows[:, None] * inv, cols[:, None] * inv], axis=1)
        cos = jnp.broadcast_to(jnp.cos(ang)[:, :, None, :], (n, 2, 2, AXIS_PAIRS)).reshape(n, ROPE_DIM)
        sin = jnp.broadcast_to(jnp.sin(ang)[:, :, None, :], (n, 2, 2, AXIS_PAIRS)).reshape(n, ROPE_DIM)
    else:
        cos = jnp.ones((n, ROPE_DIM), F32)
        sin = jnp.zeros((n, ROPE_DIM), F32)
    pad = HEAD_PAD - QK_NOPE_DIM - ROPE_DIM
    cos_h = jnp.concatenate([jnp.ones((n, QK_NOPE_DIM), F32), cos, jnp.zeros((n, pad), F32)], axis=1)
    sin_h = jnp.concatenate([jnp.zeros((n, QK_NOPE_DIM), F32), sin, jnp.zeros((n, pad), F32)], axis=1)
    qscale = MLA_SCALE * math.log2(math.e)
    cosf = jnp.tile(cos_h, (1, N_HEADS)) * qscale
    sinf = jnp.tile(sin_h, (1, N_HEADS)) * qscale
    cosk = jnp.concatenate([cos, sin, jnp.zeros((n, LANES - 2 * ROPE_DIM), F32)], axis=1)
    return cosf, sinf, cosk


def _rot_matrix():
    r = np.zeros((ROPE_DIM, ROPE_DIM), np.float32)
    for a in range(2):
        for p in range(AXIS_PAIRS):
            lo = a * 2 * AXIS_PAIRS + p
            hi = lo + AXIS_PAIRS
            r[hi, lo] = -1.0
            r[lo, hi] = 1.0
    return jnp.asarray(r)


def _prep_layer(l, w):
    rot = _rot_matrix()
    w_in = w["w_in"][l]
    o4 = 3 * HY_WIDTH + FN_WIDTH + Q_LORA_RANK + KV_LORA_RANK
    w_kr = w_in[:, o4:]
    w_all = jnp.concatenate(
        [w_in[:, :o4], w_kr, w_kr @ rot, jnp.zeros((D_MODEL, LANES - 2 * ROPE_DIM), F32)], axis=1).astype(BF16)
    hd = QK_NOPE_DIM + ROPE_DIM
    pad = HEAD_PAD - hd
    wq = w["w_uq"][l].reshape(Q_LORA_RANK, N_HEADS, hd)
    zq = jnp.zeros((Q_LORA_RANK, N_HEADS, pad), F32)
    plain = jnp.concatenate([wq, zq], axis=2)
    rotated = jnp.concatenate([jnp.zeros((Q_LORA_RANK, N_HEADS, QK_NOPE_DIM), F32),
                               jnp.einsum("rhd,de->rhe", wq[:, :, QK_NOPE_DIM:], rot), zq], axis=2)
    hw = N_HEADS * HEAD_PAD
    w_uq_aug = jnp.concatenate([plain.reshape(Q_LORA_RANK, hw), rotated.reshape(Q_LORA_RANK, hw)], axis=1).astype(BF16)
    wkv = w["w_ukv"][l].reshape(KV_LORA_RANK, N_HEADS, QK_NOPE_DIM + V_HEAD_DIM)
    wk = jnp.concatenate([wkv[:, :, :QK_NOPE_DIM],
                          jnp.zeros((KV_LORA_RANK, N_HEADS, HEAD_PAD - QK_NOPE_DIM), F32)], axis=2).reshape(KV_LORA_RANK, hw)
    sel = np.zeros((LANES, N_HEADS, HEAD_PAD), np.float32)
    for j in range(ROPE_DIM):
        sel[j, :, QK_NOPE_DIM + j] = 1.0
        sel[ROPE_DIM + j, :, QK_NOPE_DIM + j] = 1.0
    wk_aug = jnp.concatenate([wk, jnp.asarray(sel).reshape(LANES, hw)], axis=0).astype(BF16)
    wv = jnp.concatenate([wkv[:, :, QK_NOPE_DIM:],
                          jnp.zeros((KV_LORA_RANK, N_HEADS, HEAD_PAD - V_HEAD_DIM), F32)], axis=2)
    wv = wv.reshape(KV_LORA_RANK, hw).astype(BF16)
    padh = LANES - FILT_HID
    w1 = jnp.pad(w["filt_w1"][l], ((0, LANES - FILT_EMB), (0, padh)))
    b1 = jnp.pad(w["filt_b1"][l], (0, padh)).reshape(1, LANES)
    w2 = jnp.pad(w["filt_w2"][l], ((0, padh), (0, padh)))
    b2 = jnp.pad(w["filt_b2"][l], (0, padh)).reshape(1, LANES)
    nblk = HY_ORDER * 2 * HY_WIDTH // LANES
    w3 = jnp.pad(w["filt_w3"][l], ((0, padh), (0, 0))).reshape(LANES, nblk, LANES).transpose(1, 0, 2)
    freq = jnp.pad(w["filt_freq"][l], ((0, 0), (0, padh)))
    nch = HY_WIDTH // LANES
    ld = w["hy_log_decay"][l].reshape(HY_ORDER, 2, nch, LANES).transpose(0, 2, 1, 3).reshape(HY_ORDER * nch, 2, LANES)
    rw = jnp.pad(w["router_w"][l], ((0, 0), (0, LANES - N_EXPERTS)))
    rb = jnp.pad(w["router_b"][l], (0, LANES - N_EXPERTS), constant_values=-1e30).reshape(1, LANES)
    half = MXU_TILE // 2
    bgu = w["moe_b_gu"][l].reshape(N_EXPERTS, 2 * D_FF // MXU_TILE, half, 2)
    bgu = bgu.transpose(0, 1, 3, 2).reshape(N_EXPERTS, 1, 2 * D_FF)
    ew = (w["moe_w_gu_prepped"][l], bgu, w["moe_w_down"][l].astype(BF16), w["moe_b_down"][l][:, None, :])
    return {
        "w_all": w_all, "w_uq_aug": w_uq_aug, "wk_aug": wk_aug, "wv": wv,
        "g_pre": w["g_pre_mix"][l].reshape(1, D_MODEL), "g_post": w["g_post_mix"][l].reshape(1, D_MODEL),
        "g_ffn": w["g_pre_ffn"][l].reshape(1, D_MODEL), "g_post_ffn": w["g_post_ffn"][l].reshape(1, D_MODEL),
        "g_q": w["q_norm_g"][l].reshape(1, Q_LORA_RANK), "g_kv": w["kv_norm_g"][l].reshape(1, KV_LORA_RANK),
        "conv_w": w["conv_w"][l], "conv_b": w["conv_b"][l], "skip": w["hy_skip"][l],
        "filt": (w1, b1, w2, b2, w3, freq, ld),
        "w_out": w["w_out"][l].astype(BF16), "rw": rw, "rb": rb, "ew": ew,
    }


def _trunk_layer(x, mod, p, pos, ctx):
    b, n, _ = x.shape
    u_hy, u_fn, q, ckv, ks = _in_proj(x, mod, p["g_pre"], p["w_all"], p["g_q"], p["w_uq_aug"], p["g_kv"],
                                      pos["cosf"], pos["sinf"])
    spec = _hyena_spectrum(n, pos["feats"], *p["filt"], pos["hy"])
    hy = _hyena_conv(u_hy, p["conv_w"], p["conv_b"], p["skip"], spec, pos["hy"])
    fn = _fnet(u_fn, pos["fn"])
    if ctx is None:
        ckv_all, ks_all, cosk = ckv, ks, pos["cosk"]
    else:
        c_ckv, c_kr = ctx
        past = c_ckv.shape[1]
        ckv_all = jnp.concatenate([c_ckv, ckv], axis=1)
        ks_all = jnp.concatenate([jnp.pad(c_kr, ((0, 0), (0, 0), (0, LANES - ROPE_DIM))), ks], axis=1)
        ident = jnp.concatenate([jnp.ones((past, ROPE_DIM), F32), jnp.zeros((past, LANES - ROPE_DIM), F32)], axis=1)
        cosk = jnp.concatenate([ident, pos["cosk"]], axis=0)
    k, v = _kv_up(ckv_all, ks_all, cosk, p["wk_aug"], p["wv"])
    att = _attention(q, k, v)
    x1, h2, logits = _out_proj(hy, fn, att, x, mod, p["w_out"], p["g_post"], p["g_ffn"], p["rw"], p["rb"])
    x2 = _moe_and_residual(h2, logits, x1, mod, p["g_post_ffn"], p["ew"])
    return x2, ckv, ks[..., :ROPE_DIM]


def _position_tables(n, use_rope, dense):
    cosf, sinf, cosk = _rope_tables(n, use_rope)
    return {"cosf": cosf, "sinf": sinf, "cosk": cosk, "feats": _filter_feats(n),
            "hy": _dense_dft_tables(n) if dense else _fft_tables(n), "fn": _fnet_tables(n)}


def kernel(x_prompt, x_sample, cache_ckv, cache_krope, c, c_ctx, w_mod, b_mod, g_pre_mix, g_post_mix, g_pre_ffn, g_post_ffn, w_in, conv_w, conv_b, filt_w1, filt_b1, filt_w2, filt_b2, filt_w3, filt_freq, hy_log_decay, hy_skip, q_norm_g, w_uq, kv_norm_g, w_ukv, w_out, router_w, router_b, moe_w_gu, moe_b_gu, moe_w_down, moe_b_down):
    w = {"g_pre_mix": g_pre_mix, "g_post_mix": g_post_mix, "g_pre_ffn": g_pre_ffn, "g_post_ffn": g_post_ffn,
         "w_in": w_in, "conv_w": conv_w, "conv_b": conv_b, "filt_w1": filt_w1, "filt_b1": filt_b1,
         "filt_w2": filt_w2, "filt_b2": filt_b2, "filt_w3": filt_w3, "filt_freq": filt_freq,
         "hy_log_decay": hy_log_decay, "hy_skip": hy_skip, "q_norm_g": q_norm_g, "w_uq": w_uq,
         "kv_norm_g": kv_norm_g, "w_ukv": w_ukv, "w_out": w_out, "router_w": router_w, "router_b": router_b,
         "moe_w_gu": moe_w_gu, "moe_b_gu": moe_b_gu, "moe_w_down": moe_w_down, "moe_b_down": moe_b_down}
    w["moe_w_gu_prepped"] = _deinterleave_gate_up(
        moe_w_gu.reshape(DEPTH * N_EXPERTS, D_MODEL, 2 * D_FF)).reshape(DEPTH, N_EXPERTS, D_MODEL, 2 * D_FF)
    nb = c.shape[0]
    rows = 16
    cc = jnp.concatenate([c_ctx[None, :], c, jnp.zeros((rows - 1 - nb, D_MODEL), F32)], axis=0)
    mod_all = _modulation(cc, w_mod, b_mod)
    layers = [_prep_layer(l, w) for l in range(DEPTH)]
    pos_p = _position_tables(x_prompt.shape[1], False, True)
    pos_s = _position_tables(x_sample.shape[1], True, False)

    y_prompt = x_prompt
    ckvs, krs = [], []
    for l in range(DEPTH):
        mod = mod_all[l, 0:1].reshape(1, 6, D_MODEL)
        y_prompt, ckv_l, kr_l = _trunk_layer(y_prompt, mod, layers[l], pos_p, None)
        ckvs.append(ckv_l)
        krs.append(kr_l)

    y_sample = x_sample
    for l in range(DEPTH):
        mod = mod_all[l, 1:1 + nb].reshape(nb, 6, D_MODEL)
        y_sample, _, _ = _trunk_layer(y_sample, mod, layers[l], pos_s, (cache_ckv[:, l], cache_krope[:, l]))

    return (y_prompt, y_sample, jnp.stack(ckvs, axis=1), jnp.stack(krs, axis=1))
```

```python
import functools
import math

import numpy as np
import jax
import jax.numpy as jnp
from jax import lax
from jax.experimental import pallas as pl
from jax.experimental.pallas import tpu as pltpu

F32 = jnp.float32
BF16 = jnp.bfloat16

D_MODEL = 1024
DEPTH = 2
GRID_W = 64
HY_WIDTH = 256
HY_ORDER = 2
FILT_BANDS = 16
FILT_EMB = 2 * FILT_BANDS + 1
FILT_HID = 64
FN_WIDTH = 256
FN_GROUP_DIM = 64
N_HEADS = 8
QK_NOPE_DIM = 64
ROPE_DIM = 32
V_HEAD_DIM = 64
Q_LORA_RANK = 256
KV_LORA_RANK = 128
MLA_SCALE = (QK_NOPE_DIM + ROPE_DIM) ** -0.5
AXIS_PAIRS = ROPE_DIM // 4
ROPE_BASE = 10000.0
N_EXPERTS = 32
TOP_K = 4
D_FF = 1024
SWIGLU_LIMIT = 7.0
SWIGLU_ALPHA = 1.702
EPS = 1e-6

LANES = 128
VMEM_LIMIT = 56 * 1024 * 1024
ROW_TILE = 256
ATTN_Q_TILE = 256
ROUTE_TILE = 512
MOE_BLOCK = 512
DMA_ROWS = 1024
DMA_UNROLL = 4
FNET_TILE = 512
HEAD_PAD = 128
FFT_N1 = 64
FFT_N2 = 128
FFT_UNROLL = 8
FN_RADIX = 64
MXU_TILE = 256
IN_SLAB = 3 * HY_WIDTH + FN_WIDTH + Q_LORA_RANK + KV_LORA_RANK + LANES


def _cparams(*sem):
    return pltpu.CompilerParams(dimension_semantics=tuple(sem), vmem_limit_bytes=VMEM_LIMIT)


def _rms(x, g):
    return x * lax.rsqrt(jnp.mean(x * x, axis=-1, keepdims=True) + EPS) * g


def _dot(a, b):
    return jnp.dot(a, b, preferred_element_type=F32)


def _dot_hi(a, b):
    return jnp.dot(a, b, preferred_element_type=F32, precision=lax.Precision.HIGHEST)


def _mod_kernel(c_ref, w_ref, b_ref, o_ref):
    c = c_ref[...]
    s = c * jax.nn.sigmoid(c)
    o_ref[0] = _dot_hi(s, w_ref[0]) + b_ref[0]


def _modulation(cc, w_mod, b_mod):
    r = cc.shape[0]
    tn = 1536
    return pl.pallas_call(
        _mod_kernel,
        grid=(DEPTH, 6 * D_MODEL // tn),
        in_specs=[pl.BlockSpec((r, D_MODEL), lambda l, j: (0, 0)),
                  pl.BlockSpec((1, D_MODEL, tn), lambda l, j: (l, 0, j)),
                  pl.BlockSpec((1, 1, tn), lambda l, j: (l, 0, j))],
        out_specs=pl.BlockSpec((1, r, tn), lambda l, j: (l, 0, j)),
        out_shape=jax.ShapeDtypeStruct((DEPTH, r, 6 * D_MODEL), F32),
        compiler_params=_cparams("parallel", "parallel"),
        name="modulation",
    )(cc, w_mod, b_mod.reshape(DEPTH, 1, 6 * D_MODEL))


def _in_proj_kernel(x_ref, mod_ref, gpre_ref, win_ref, gq_ref, wuq_ref, gkv_ref, cosf_ref, sinf_ref,
                    hy_ref, fn_ref, q_ref, ckv_ref, ks_ref):
    x = x_ref[0]
    sh1 = mod_ref[0, 0:1, :]
    sc1 = mod_ref[0, 1:2, :]
    h = _rms(x, gpre_ref[...]) * (1.0 + sc1) + sh1
    proj = _dot(h.astype(BF16), win_ref[...])
    o1 = 3 * HY_WIDTH
    o2 = o1 + FN_WIDTH
    o3 = o2 + Q_LORA_RANK
    o4 = o3 + KV_LORA_RANK
    hy_ref[0] = proj[:, :o1].astype(BF16)
    fn_ref[0] = proj[:, o1:o2].astype(BF16)
    qlat = _rms(proj[:, o2:o3], gq_ref[...]).astype(BF16)
    q2 = _dot(qlat, wuq_ref[...])
    hw = N_HEADS * HEAD_PAD
    q = q2[:, :hw] * cosf_ref[...] + q2[:, hw:] * sinf_ref[...]
    q_ref[0] = q.astype(BF16)
    ckv_ref[0] = _rms(proj[:, o3:o4], gkv_ref[...])
    ks_ref[0] = proj[:, o4:]


def _in_proj(x, mod, g_pre, w_all, g_q, w_uq_aug, g_kv, cosf, sinf):
    b, n, _ = x.shape
    tm = ROW_TILE
    hw = N_HEADS * HEAD_PAD
    mod_map = (lambda bi, i: (bi, 0, 0)) if mod.shape[0] > 1 else (lambda bi, i: (0, 0, 0))
    const = lambda bi, i: (0, 0)
    return pl.pallas_call(
        _in_proj_kernel,
        grid=(b, n // tm),
        in_specs=[pl.BlockSpec((1, tm, D_MODEL), lambda bi, i: (bi, i, 0)),
                  pl.BlockSpec((1, 6, D_MODEL), mod_map),
                  pl.BlockSpec((1, D_MODEL), const),
                  pl.BlockSpec((D_MODEL, IN_SLAB), const),
                  pl.BlockSpec((1, Q_LORA_RANK), const),
                  pl.BlockSpec((Q_LORA_RANK, 2 * hw), const),
                  pl.BlockSpec((1, KV_LORA_RANK), const),
                  pl.BlockSpec((tm, hw), lambda bi, i: (i, 0)),
                  pl.BlockSpec((tm, hw), lambda bi, i: (i, 0))],
        out_specs=[pl.BlockSpec((1, tm, 3 * HY_WIDTH), lambda bi, i: (bi, i, 0)),
                   pl.BlockSpec((1, tm, FN_WIDTH), lambda bi, i: (bi, i, 0)),
                   pl.BlockSpec((1, tm, hw), lambda bi, i: (bi, i, 0)),
                   pl.BlockSpec((1, tm, KV_LORA_RANK), lambda bi, i: (bi, i, 0)),
                   pl.BlockSpec((1, tm, LANES), lambda bi, i: (bi, i, 0))],
        out_shape=[jax.ShapeDtypeStruct((b, n, 3 * HY_WIDTH), BF16),
                   jax.ShapeDtypeStruct((b, n, FN_WIDTH), BF16),
                   jax.ShapeDtypeStruct((b, n, hw), BF16),
                   jax.ShapeDtypeStruct((b, n, KV_LORA_RANK), F32),
                   jax.ShapeDtypeStruct((b, n, LANES), F32)],
        compiler_params=_cparams("parallel", "parallel"),
        name="in_proj",
    )(x, mod, g_pre, w_all, g_q, w_uq_aug, g_kv, cosf, sinf)


def _filter_taps(feats_ref, w1_ref, b1_ref, w2_ref, b2_ref, w3f_ref, w3b_ref, freq_ref, ld_ref, hid_ref):
    rows = feats_ref.shape[0]
    n = rows // 2

    @pl.when((pl.program_id(0) == 0) & (pl.program_id(1) == 0))
    def _():
        tr = min(rows, 1024)

        def chunk(i, carry):
            r0 = pl.multiple_of(i * tr, tr)
            h1 = jnp.sin(freq_ref[0:1, :] * (_dot_hi(feats_ref[pl.ds(r0, tr), :], w1_ref[...]) + b1_ref[...]))
            hid_ref[pl.ds(r0, tr), :] = jnp.sin(freq_ref[1:2, :] * (_dot_hi(h1, w2_ref[...]) + b2_ref[...]))
            return carry
        lax.fori_loop(0, rows // tr, chunk, 0)

    hf = (_dot_hi(hid_ref[pl.ds(0, n), :], w3f_ref[0])
          * jnp.exp(-jnp.exp(ld_ref[0, 0:1, :]) * feats_ref[pl.ds(0, n), 0:1]))
    hb = (_dot_hi(hid_ref[pl.ds(n, n), :], w3b_ref[0])
          * jnp.exp(-jnp.exp(ld_ref[0, 1:2, :]) * feats_ref[pl.ds(n, n), 0:1]))
    hb = jnp.where(lax.broadcasted_iota(jnp.int32, hb.shape, 0) == 0, 0.0, hb)
    ss = jnp.sum(hf * hf, axis=0, keepdims=True) + jnp.sum(hb * hb, axis=0, keepdims=True)
    inv = lax.rsqrt(ss + EPS)
    return hf * inv, hb * inv


def _filter_dense_kernel(feats_ref, w1_ref, b1_ref, w2_ref, b2_ref, w3f_ref, w3b_ref, freq_ref, ld_ref,
                         md_ref, spec_ref, hid_ref):
    kf, kb = _filter_taps(feats_ref, w1_ref, b1_ref, w2_ref, b2_ref, w3f_ref, w3b_ref, freq_ref, ld_ref, hid_ref)
    k = jnp.concatenate([kf, kb], axis=0)
    nn = k.shape[0]
    z = _dot(md_ref[...], k.astype(BF16))
    spec_ref[0, 0] = z[:nn].astype(BF16)
    spec_ref[0, 1] = z[nn:].astype(BF16)


def _fft_outer_forward(src_ref, m1_ref, bre_ref, bim_ref, k1):
    def body(s2, carry):
        xs = src_ref[pl.ds(s2, k1, stride=FFT_N2), :].astype(BF16)
        r = _dot(m1_ref[s2], xs)
        base = pl.multiple_of(s2 * FFT_N1, FFT_N1)
        bre_ref[pl.ds(base, FFT_N1), :] = r[:FFT_N1]
        bim_ref[pl.ds(base, FFT_N1), :] = r[FFT_N1:]
        return carry
    lax.fori_loop(0, FFT_N2, body, 0, unroll=FFT_UNROLL)


def _fft_inner_block(bre_ref, bim_ref, mf_ref, f1):
    xr = bre_ref[pl.ds(f1, FFT_N2, stride=FFT_N1), :]
    xi = bim_ref[pl.ds(f1, FFT_N2, stride=FFT_N1), :]
    xx = jnp.concatenate([xr, xi], axis=0).astype(BF16)
    z = _dot(mf_ref[...], xx)
    return z[:FFT_N2], z[FFT_N2:]


def _filter_fft_kernel(feats_ref, w1_ref, b1_ref, w2_ref, b2_ref, w3f_ref, w3b_ref, freq_ref, ld_ref,
                       m1_ref, mf_ref, spec_ref, k_ref, bre_ref, bim_ref, hid_ref):
    kf, kb = _filter_taps(feats_ref, w1_ref, b1_ref, w2_ref, b2_ref, w3f_ref, w3b_ref, freq_ref, ld_ref, hid_ref)
    n = kf.shape[0]
    k_ref[pl.ds(0, n), :] = kf
    k_ref[pl.ds(n, n), :] = kb
    _fft_outer_forward(k_ref, m1_ref, bre_ref, bim_ref, FFT_N1)

    def body(f1, carry):
        zr, zi = _fft_inner_block(bre_ref, bim_ref, mf_ref, f1)
        base = pl.multiple_of(f1 * FFT_N2, FFT_N2)
        spec_ref[0, 0, pl.ds(base, FFT_N2), :] = zr.astype(BF16)
        spec_ref[0, 1, pl.ds(base, FFT_N2), :] = zi.astype(BF16)
        return carry
    lax.fori_loop(0, FFT_N1, body, 0, unroll=FFT_UNROLL // 2)


def _hyena_spectrum(n, feats, w1, b1, w2, b2, w3, freq, ld, tables):
    nn = 2 * n
    nchunk = HY_WIDTH // LANES
    c2 = lambda o, c: (0, 0)
    one = pl.Buffered(1)
    in_specs = [pl.BlockSpec((nn, LANES), c2, pipeline_mode=one),
                pl.BlockSpec((LANES, LANES), c2), pl.BlockSpec((1, LANES), c2),
                pl.BlockSpec((LANES, LANES), c2), pl.BlockSpec((1, LANES), c2),
                pl.BlockSpec((1, LANES, LANES), lambda o, c: (o * 2 * nchunk + c, 0, 0)),
                pl.BlockSpec((1, LANES, LANES), lambda o, c: (o * 2 * nchunk + nchunk + c, 0, 0)),
                pl.BlockSpec((2, LANES), c2),
                pl.BlockSpec((1, 2, LANES), lambda o, c: (o * nchunk + c, 0, 0))]
    args = [feats, w1, b1, w2, b2, w3, w3, freq, ld]
    out_spec = pl.BlockSpec((1, 2, nn, LANES), lambda o, c: (o, 0, 0, c))
    out_shape = jax.ShapeDtypeStruct((HY_ORDER, 2, nn, HY_WIDTH), BF16)
    if "md" in tables:
        return pl.pallas_call(
            _filter_dense_kernel, grid=(HY_ORDER, nchunk),
            in_specs=in_specs + [pl.BlockSpec((2 * nn, nn), c2)],
            out_specs=out_spec, out_shape=out_shape,
            scratch_shapes=[pltpu.VMEM((nn, LANES), F32)],
            compiler_params=_cparams("arbitrary", "arbitrary"), name="hyena_filter_dense",
        )(*args, tables["md"])
    return pl.pallas_call(
        _filter_fft_kernel, grid=(HY_ORDER, nchunk),
        in_specs=in_specs + [pl.BlockSpec((FFT_N2, 2 * FFT_N1, FFT_N1), lambda o, c: (0, 0, 0), pipeline_mode=one),
                             pl.BlockSpec((2 * FFT_N2, 2 * FFT_N2), c2)],
        out_specs=out_spec, out_shape=out_shape,
        scratch_shapes=[pltpu.VMEM((nn, LANES), F32)] * 4,
        compiler_params=_cparams("arbitrary", "arbitrary"), name="hyena_filter_fft",
    )(*args, tables["m1_full"], tables["mf"])


def _short_conv(u, cw, cb):
    n = u.shape[0]
    row = lax.broadcasted_iota(jnp.int32, u.shape, 0)
    prev = jnp.where(row == 0, 0.0, pltpu.roll(u, 1, 0))
    nxt = jnp.where(row == n - 1, 0.0, pltpu.roll(u, n - 1, 0))
    return prev * cw[0:1, :] + u * cw[1:2, :] + nxt * cw[2:3, :] + cb


def _hyena_dense_kernel(x1_ref, x2_ref, v_ref, cw1_ref, cw2_ref, cwv_ref, cb1_ref, cb2_ref, cbv_ref, skip_ref,
                        spec_ref, md_ref, mdi_ref, o_ref):
    n = v_ref.shape[1]
    nn = 2 * n
    z = _short_conv(v_ref[0].astype(F32), cwv_ref[...], cbv_ref[...])
    gates = ((x1_ref, cw1_ref, cb1_ref), (x2_ref, cw2_ref, cb2_ref))
    for o, (g_ref, cw_ref, cb_ref) in enumerate(gates):
        zf = _dot(md_ref[...], z.astype(BF16))
        zr, zi = zf[:nn], zf[nn:]
        sr = spec_ref[o, 0].astype(F32)
        si = spec_ref[o, 1].astype(F32)
        yy = jnp.concatenate([zr * sr - zi * si, zr * si + zi * sr], axis=0).astype(BF16)
        y = _dot(mdi_ref[...], yy) * (1.0 / nn)
        gate = _short_conv(g_ref[0].astype(F32), cw_ref[...], cb_ref[...])
        z = gate * (y + skip_ref[o:o + 1, :] * z)
    o_ref[0] = z.astype(BF16)


def _hyena_fft_kernel(x1_ref, x2_ref, v_ref, cw1_ref, cw2_ref, cwv_ref, cb1_ref, cb2_ref, cbv_ref, skip_ref,
                      spec_ref, m1_ref, mf_ref, mi_ref, m3_ref, o_ref,
                      z_ref, bre_ref, bim_ref, pre_ref, pim_ref):
    n = v_ref.shape[1]
    nn = 2 * n
    k1 = FFT_N1 // 2
    z_ref[...] = _short_conv(v_ref[0].astype(F32), cwv_ref[...], cbv_ref[...])
    gates = ((x1_ref, cw1_ref, cb1_ref), (x2_ref, cw2_ref, cb2_ref))
    for o, (g_ref, cw_ref, cb_ref) in enumerate(gates):
        _fft_outer_forward(z_ref, m1_ref, bre_ref, bim_ref, k1)

        def mid(f1, carry):
            zr, zi = _fft_inner_block(bre_ref, bim_ref, mf_ref, f1)
            base = pl.multiple_of(f1 * FFT_N2, FFT_N2)
            sr = spec_ref[o, 0, pl.ds(base, FFT_N2), :].astype(F32)
            si = spec_ref[o, 1, pl.ds(base, FFT_N2), :].astype(F32)
            yy = jnp.concatenate([zr * sr - zi * si, zr * si + zi * sr], axis=0).astype(BF16)
            p = _dot(mi_ref[...], yy)
            pre_ref[pl.ds(base, FFT_N2), :] = p[:FFT_N2]
            pim_ref[pl.ds(base, FFT_N2), :] = p[FFT_N2:]
            return carry
        lax.fori_loop(0, FFT_N1, mid, 0, unroll=FFT_UNROLL)

        def last(t2, carry):
            pr = pre_ref[pl.ds(t2, FFT_N1, stride=FFT_N2), :]
            pi_ = pim_ref[pl.ds(t2, FFT_N1, stride=FFT_N2), :]
            xx = jnp.concatenate([pr, pi_], axis=0).astype(BF16)
            yv = _dot(m3_ref[t2], xx) * (1.0 / nn)
            bre_ref[pl.ds(t2, k1, stride=FFT_N2), :] = yv
            return carry
        lax.fori_loop(0, FFT_N2, last, 0, unroll=FFT_UNROLL)

        gate = _short_conv(g_ref[0].astype(F32), cw_ref[...], cb_ref[...])
        z_ref[...] = gate * (bre_ref[pl.ds(0, n), :] + skip_ref[o:o + 1, :] * z_ref[...])
    o_ref[0] = z_ref[...].astype(BF16)


def _hyena_conv(u_hy, conv_w, conv_b, skip, spec, tables):
    b, n, _ = u_hy.shape
    nn = 2 * n
    nchunk = HY_WIDTH // LANES
    cb = conv_b.reshape(1, 3 * HY_WIDTH)

    def part(p):
        return pl.BlockSpec((1, n, LANES), lambda c, bi: (bi, 0, p * nchunk + c))

    def cwp(p):
        return pl.BlockSpec((3, LANES), lambda c, bi: (0, p * nchunk + c))

    def cbp(p):
        return pl.BlockSpec((1, LANES), lambda c, bi: (0, p * nchunk + c))

    in_specs = [part(0), part(1), part(2), cwp(0), cwp(1), cwp(2), cbp(0), cbp(1), cbp(2),
                pl.BlockSpec((HY_ORDER, LANES), lambda c, bi: (0, c)),
                pl.BlockSpec((HY_ORDER, 2, nn, LANES), lambda c, bi: (0, 0, 0, c), pipeline_mode=pl.Buffered(1))]
    args = [u_hy, u_hy, u_hy, conv_w, conv_w, conv_w, cb, cb, cb, skip, spec]
    out_spec = pl.BlockSpec((1, n, LANES), lambda c, bi: (bi, 0, c))
    out_shape = jax.ShapeDtypeStruct((b, n, HY_WIDTH), BF16)
    c2 = lambda c, bi: (0, 0)
    c3 = lambda c, bi: (0, 0, 0)
    if "md" in tables:
        return pl.pallas_call(
            _hyena_dense_kernel, grid=(nchunk, b),
            in_specs=in_specs + [pl.BlockSpec((2 * nn, n), c2), pl.BlockSpec((n, 2 * nn), c2)],
            out_specs=out_spec, out_shape=out_shape,
            compiler_params=_cparams("parallel", "parallel"), name="hyena_conv_dense",
        )(*args, tables["md_half"], tables["mdi"])
    one = pl.Buffered(1)
    return pl.pallas_call(
        _hyena_fft_kernel, grid=(nchunk, b),
        in_specs=in_specs + [pl.BlockSpec((FFT_N2, 2 * FFT_N1, FFT_N1 // 2), c3, pipeline_mode=one),
                             pl.BlockSpec((2 * FFT_N2, 2 * FFT_N2), c2),
                             pl.BlockSpec((2 * FFT_N2, 2 * FFT_N2), c2),
                             pl.BlockSpec((FFT_N2, FFT_N1 // 2, 2 * FFT_N1), c3, pipeline_mode=one)],
        out_specs=out_spec, out_shape=out_shape,
        scratch_shapes=[pltpu.VMEM((n, LANES), F32)] + [pltpu.VMEM((nn, LANES), F32)] * 4,
        compiler_params=_cparams("parallel", "parallel"), name="hyena_conv_fft",
    )(*args, tables["m1_half"], tables["mf"], tables["mi"], tables["m3"])


def _fnet_kernel(u_ref, bc_ref, bs_ref, lhs_ref, o_ref, xcs_ref):
    n = u_ref.shape[1]

    @pl.when(pl.program_id(1) == 0)
    def _():
        u = u_ref[0]
        xcs_ref[pl.ds(0, n), :] = _dot(u, bc_ref[...]).astype(BF16)
        xcs_ref[pl.ds(n, n), :] = _dot(u, bs_ref[...]).astype(BF16)

    scale = 1.0 / math.sqrt(n * FN_GROUP_DIM)
    o_ref[0] = (_dot(lhs_ref[...], xcs_ref[...]) * scale).astype(BF16)


def _fnet_fft_kernel(u_ref, bc_ref, bs_ref, m1_ref, m2_ref, o_ref, wre_ref, wim_ref, bre_ref, bim_ref):
    n = u_ref.shape[1]
    r = FN_RADIX
    u = u_ref[0]
    wre_ref[...] = _dot(u, bc_ref[...])
    wim_ref[...] = -_dot(u, bs_ref[...])

    def outer(s2, carry):
        xx = jnp.concatenate([wre_ref[pl.ds(s2, r, stride=r), :], wim_ref[pl.ds(s2, r, stride=r), :]], axis=0)
        z = _dot(m1_ref[s2], xx.astype(BF16))
        base = pl.multiple_of(s2 * r, r)
        bre_ref[pl.ds(base, r), :] = z[:r]
        bim_ref[pl.ds(base, r), :] = z[r:]
        return carry
    lax.fori_loop(0, r, outer, 0, unroll=FFT_UNROLL)

    scale = 1.0 / math.sqrt(n * FN_GROUP_DIM)

    def inner(f1, carry):
        xx = jnp.concatenate([bre_ref[pl.ds(f1, r, stride=r), :], bim_ref[pl.ds(f1, r, stride=r), :]], axis=0)
        wre_ref[pl.ds(f1, r, stride=r), :] = _dot(m2_ref[...], xx.astype(BF16)) * scale
        return carry
    lax.fori_loop(0, r, inner, 0, unroll=FFT_UNROLL)
    o_ref[0] = wre_ref[...].astype(BF16)


def _fnet_fft(u_fn, tables):
    b, n, _ = u_fn.shape
    r = FN_RADIX
    nchunk = FN_WIDTH // LANES
    return pl.pallas_call(
        _fnet_fft_kernel, grid=(b, nchunk),
        in_specs=[pl.BlockSpec((1, n, FN_WIDTH), lambda bi, c: (bi, 0, 0)),
                  pl.BlockSpec((FN_WIDTH, LANES), lambda bi, c: (0, c)),
                  pl.BlockSpec((FN_WIDTH, LANES), lambda bi, c: (0, c)),
                  pl.BlockSpec((r, 2 * r, 2 * r), lambda bi, c: (0, 0, 0)),
                  pl.BlockSpec((r, 2 * r), lambda bi, c: (0, 0))],
        out_specs=pl.BlockSpec((1, n, LANES), lambda bi, c: (bi, 0, c)),
        out_shape=jax.ShapeDtypeStruct((b, n, FN_WIDTH), BF16),
        scratch_shapes=[pltpu.VMEM((n, LANES), F32)] * 4,
        compiler_params=_cparams("parallel", "parallel"), name="fnet_fft",
    )(u_fn, tables["bc"], tables["bs"], tables["fn_m1"], tables["fn_m2"])


def _fnet(u_fn, tables):
    if "fn_m1" in tables:
        return _fnet_fft(u_fn, tables)
    b, n, _ = u_fn.shape
    tq = min(FNET_TILE, n)
    return pl.pallas_call(
        _fnet_kernel, grid=(b, n // tq),
        in_specs=[pl.BlockSpec((1, n, FN_WIDTH), lambda bi, i: (bi, 0, 0)),
                  pl.BlockSpec((FN_WIDTH, FN_WIDTH), lambda bi, i: (0, 0)),
                  pl.BlockSpec((FN_WIDTH, FN_WIDTH), lambda bi, i: (0, 0)),
                  pl.BlockSpec((tq, 2 * n), lambda bi, i: (i, 0))],
        out_specs=pl.BlockSpec((1, tq, FN_WIDTH), lambda bi, i: (bi, i, 0)),
        out_shape=jax.ShapeDtypeStruct((b, n, FN_WIDTH), BF16),
        scratch_shapes=[pltpu.VMEM((2 * n, FN_WIDTH), BF16)],
        compiler_params=_cparams("parallel", "arbitrary"), name="fnet",
    )(u_fn, tables["bc"], tables["bs"], tables["fn_lhs"])


def _kv_kernel(ckv_ref, ks_ref, cosk_ref, wk_ref, wv_ref, k_ref, v_ref):
    ckv = ckv_ref[0].astype(BF16)
    kr = (ks_ref[0] * cosk_ref[...]).astype(BF16)
    k_ref[0] = _dot(jnp.concatenate([ckv, kr], axis=-1), wk_ref[...]).astype(BF16)
    v = _dot(ckv, wv_ref[...])
    lane = lax.broadcasted_iota(jnp.int32, v.shape, 1)
    v_ref[0] = jnp.where(lane % HEAD_PAD == V_HEAD_DIM, 1.0, v).astype(BF16)


def _kv_up(ckv_all, ks_all, cosk, wk_aug, wv):
    b, lk, _ = ckv_all.shape
    tk = ROW_TILE
    hw = N_HEADS * HEAD_PAD
    vw = N_HEADS * HEAD_PAD
    return pl.pallas_call(
        _kv_kernel, grid=(b, lk // tk),
        in_specs=[pl.BlockSpec((1, tk, KV_LORA_RANK), lambda bi, i: (bi, i, 0)),
                  pl.BlockSpec((1, tk, LANES), lambda bi, i: (bi, i, 0)),
                  pl.BlockSpec((tk, LANES), lambda bi, i: (i, 0)),
                  pl.BlockSpec((KV_LORA_RANK + LANES, hw), lambda bi, i: (0, 0)),
                  pl.BlockSpec((KV_LORA_RANK, vw), lambda bi, i: (0, 0))],
        out_specs=[pl.BlockSpec((1, tk, hw), lambda bi, i: (bi, i, 0)),
                   pl.BlockSpec((1, tk, vw), lambda bi, i: (bi, i, 0))],
        out_shape=[jax.ShapeDtypeStruct((b, lk, hw), BF16), jax.ShapeDtypeStruct((b, lk, vw), BF16)],
        compiler_params=_cparams("parallel", "parallel"), name="kv_up",
    )(ckv_all, ks_all, cosk, wk_aug, wv)


def _attn_kernel(q_ref, k_ref, v_ref, o_ref):
    outs = []
    for h in range(N_HEADS):
        q = q_ref[0, :, h * HEAD_PAD:(h + 1) * HEAD_PAD]
        k = k_ref[0, :, h * HEAD_PAD:(h + 1) * HEAD_PAD]
        s = lax.dot_general(q, k, (((1,), (1,)), ((), ())), preferred_element_type=F32)
        m = jnp.max(s, axis=-1, keepdims=True)
        p = jnp.exp2((s - m).astype(BF16))
        ov = _dot(p, v_ref[0, :, h * HEAD_PAD:(h + 1) * HEAD_PAD])
        outs.append(ov[:, :V_HEAD_DIM] * (1.0 / ov[:, V_HEAD_DIM:V_HEAD_DIM + 1]))
    o_ref[0] = jnp.concatenate(outs, axis=-1).astype(BF16)


def _attention(q, k, v):
    b, n, hw = q.shape
    lk = k.shape[1]
    vw = N_HEADS * V_HEAD_DIM
    tq = ATTN_Q_TILE
    one = pl.Buffered(1)
    return pl.pallas_call(
        _attn_kernel, grid=(b, n // tq),
        in_specs=[pl.BlockSpec((1, tq, hw), lambda bi, i: (bi, i, 0)),
                  pl.BlockSpec((1, lk, hw), lambda bi, i: (bi, 0, 0), pipeline_mode=one),
                  pl.BlockSpec((1, lk, hw), lambda bi, i: (bi, 0, 0), pipeline_mode=one)],
        out_specs=pl.BlockSpec((1, tq, vw), lambda bi, i: (bi, i, 0)),
        out_shape=jax.ShapeDtypeStruct((b, n, vw), BF16),
        compiler_params=_cparams("parallel", "parallel"), name="attention",
    )(q, k, v)


def _out_proj_kernel(hy_ref, fn_ref, att_ref, x_ref, mod_ref, wo_ref, gpost_ref, gffn_ref, rw_ref, rb_ref,
                     x1_ref, h2_ref, lg_ref):
    o1 = HY_WIDTH
    o2 = o1 + FN_WIDTH
    m = (_dot(hy_ref[0], wo_ref[:o1, :]) + _dot(fn_ref[0], wo_ref[o1:o2, :]) + _dot(att_ref[0], wo_ref[o2:, :]))
    g1 = mod_ref[0, 2:3, :]
    sh2 = mod_ref[0, 3:4, :]
    sc2 = mod_ref[0, 4:5, :]
    x1 = x_ref[0] + g1 * _rms(m, gpost_ref[...])
    x1_ref[0] = x1
    h2 = _rms(x1, gffn_ref[...]) * (1.0 + sc2) + sh2
    h2_ref[0] = h2
    lg_ref[0] = _dot_hi(h2, rw_ref[...]) + rb_ref[...]


def _out_proj(hy, fn, att, x, mod, w_out, g_post, g_ffn, rw_pad, rb_pad):
    b, n, _ = x.shape
    tm = ROW_TILE
    mix = HY_WIDTH + FN_WIDTH + N_HEADS * V_HEAD_DIM
    mod_map = (lambda bi, i: (bi, 0, 0)) if mod.shape[0] > 1 else (lambda bi, i: (0, 0, 0))
    const = lambda bi, i: (0, 0)
    row = lambda w: pl.BlockSpec((1, tm, w), lambda bi, i: (bi, i, 0))
    return pl.pallas_call(
        _out_proj_kernel, grid=(b, n // tm),
        in_specs=[row(HY_WIDTH), row(FN_WIDTH), row(N_HEADS * V_HEAD_DIM), row(D_MODEL),
                  pl.BlockSpec((1, 6, D_MODEL), mod_map),
                  pl.BlockSpec((mix, D_MODEL), const),
                  pl.BlockSpec((1, D_MODEL), const), pl.BlockSpec((1, D_MODEL), const),
                  pl.BlockSpec((D_MODEL, LANES), const), pl.BlockSpec((1, LANES), const)],
        out_specs=[row(D_MODEL), row(D_MODEL), row(LANES)],
        out_shape=[jax.ShapeDtypeStruct((b, n, D_MODEL), F32),
                   jax.ShapeDtypeStruct((b, n, D_MODEL), F32),
                   jax.ShapeDtypeStruct((b, n, LANES), F32)],
        compiler_params=_cparams("parallel", "parallel"), name="out_proj",
    )(hy, fn, att, x, mod, w_out, g_post, g_ffn, rw_pad, rb_pad)


def _route_kernel(lg_ref, info_ref, cnt_ref, carry_ref):
    i = pl.program_id(0)

    @pl.when(i == 0)
    def _():
        carry_ref[...] = jnp.zeros_like(carry_ref)

    v = lg_ref[...]
    tm = v.shape[0]
    lane = lax.broadcasted_iota(jnp.int32, v.shape, 1)
    sels, vals, idxs = [], [], []
    for _ in range(TOP_K):
        m = jnp.max(v, axis=-1, keepdims=True)
        idx = jnp.min(jnp.where(v == m, lane, LANES), axis=-1, keepdims=True)
        sel = lane == idx
        sels.append(sel)
        vals.append(m)
        idxs.append(idx)
        v = jnp.where(sel, -jnp.inf, v)
    es = [jnp.exp(val - vals[0]) for val in vals]
    den = es[0] + es[1] + es[2] + es[3]
    onehot = jnp.where(sels[0] | sels[1] | sels[2] | sels[3], 1.0, 0.0)
    r_i = lax.broadcasted_iota(jnp.int32, (tm, tm), 0)
    c_i = lax.broadcasted_iota(jnp.int32, (tm, tm), 1)
    tri = jnp.where(c_i < r_i, 1.0, 0.0).astype(BF16)
    rank = _dot(tri, onehot.astype(BF16)) + carry_ref[0:1, :]
    info = jnp.zeros(v.shape, F32)
    for kk in range(TOP_K):
        rk = jnp.sum(jnp.where(sels[kk], rank, 0.0), axis=-1, keepdims=True)
        info = jnp.where(lane == kk, es[kk] / den, info)
        info = jnp.where(lane == TOP_K + kk, idxs[kk].astype(F32), info)
        info = jnp.where(lane == 2 * TOP_K + kk, rk, info)
    info_ref[...] = info
    carry_ref[0:1, :] = carry_ref[0:1, :] + jnp.sum(onehot, axis=0, keepdims=True)
    cnt_ref[...] = carry_ref[...]


def _route(logits):
    t = logits.shape[0]
    tm = ROUTE_TILE
    return pl.pallas_call(
        _route_kernel, grid=(t // tm,),
        in_specs=[pl.BlockSpec((tm, LANES), lambda i: (i, 0))],
        out_specs=[pl.BlockSpec((tm, LANES), lambda i: (i, 0)), pl.BlockSpec((8, LANES), lambda i: (0, 0))],
        out_shape=[jax.ShapeDtypeStruct((t, LANES), F32), jax.ShapeDtypeStruct((8, LANES), F32)],
        scratch_shapes=[pltpu.VMEM((8, LANES), F32)],
        compiler_params=_cparams("arbitrary"), name="route",
    )(logits)


def _dispatch_kernel(dest_ref, h_ref, xb_in_ref, xb_ref, idx_ref, isem, sem):
    del xb_in_ref
    i = pl.program_id(0)
    rows = h_ref.shape[0]
    cp = pltpu.make_async_copy(dest_ref.at[i], idx_ref, isem)
    cp.start()
    cp.wait()

    def row_copy(r, kk, slot):
        return pltpu.make_async_copy(h_ref.at[pl.ds(r, 1)], xb_ref.at[pl.ds(slot, 1)], sem)

    def issue(r, carry):
        for kk in range(TOP_K):
            row_copy(r, kk, idx_ref[r * TOP_K + kk]).start()
        return carry
    lax.fori_loop(0, rows, issue, 0, unroll=DMA_UNROLL)

    def drain(r, carry):
        for kk in range(TOP_K):
            row_copy(r, kk, 0).wait()
        return carry
    lax.fori_loop(0, rows, drain, 0, unroll=DMA_UNROLL)


def _dispatch(h2, dest, n_slots):
    t = h2.shape[0]
    rows = DMA_ROWS
    xb0 = jnp.zeros((n_slots, D_MODEL), F32)
    return pl.pallas_call(
        _dispatch_kernel, grid=(t // rows,),
        in_specs=[pl.BlockSpec(memory_space=pl.ANY),
                  pl.BlockSpec((rows, D_MODEL), lambda i: (i, 0)),
                  pl.BlockSpec(memory_space=pl.ANY)],
        out_specs=pl.BlockSpec(memory_space=pl.ANY),
        out_shape=jax.ShapeDtypeStruct((n_slots, D_MODEL), F32),
        scratch_shapes=[pltpu.SMEM((rows * TOP_K,), jnp.int32),
                        pltpu.SemaphoreType.DMA(()), pltpu.SemaphoreType.DMA(())],
        input_output_aliases={2: 0},
        compiler_params=_cparams("arbitrary"), name="moe_dispatch",
    )(dest.reshape(t // rows, rows * TOP_K), h2, xb0)


def _deinterleave_kernel(w_ref, s_ref, o_ref):
    for k in range(w_ref.shape[2] // MXU_TILE):
        blk = w_ref[0, :, k * MXU_TILE:(k + 1) * MXU_TILE].astype(BF16)
        o_ref[0, :, k * MXU_TILE:(k + 1) * MXU_TILE] = _dot(blk, s_ref[...]).astype(BF16)


def _deinterleave_gate_up(w_gu):
    ne, d, f2 = w_gu.shape
    tr = 512
    half = MXU_TILE // 2
    sel = np.zeros((MXU_TILE, MXU_TILE), np.float32)
    for j in range(half):
        sel[2 * j, j] = 1.0
        sel[2 * j + 1, half + j] = 1.0
    return pl.pallas_call(
        _deinterleave_kernel, grid=(ne, d // tr),
        in_specs=[pl.BlockSpec((1, tr, f2), lambda e, i: (e, i, 0)),
                  pl.BlockSpec((MXU_TILE, MXU_TILE), lambda e, i: (0, 0))],
        out_specs=pl.BlockSpec((1, tr, f2), lambda e, i: (e, i, 0)),
        out_shape=jax.ShapeDtypeStruct((ne, d, f2), BF16),
        compiler_params=_cparams("parallel", "parallel"), name="moe_weight_prep",
    )(w_gu, jnp.asarray(sel, BF16))


def _expert_kernel(be_ref, nu_ref, x_ref, wgu_ref, bgu_ref, wd_ref, bd_ref, o_ref):
    i = pl.program_id(0)

    @pl.when(i < nu_ref[0])
    def _():
        x = x_ref[...].astype(BF16)
        gu = _dot(x, wgu_ref[0]) + bgu_ref[0]
        half = MXU_TILE // 2
        acts = []
        for k in range(gu.shape[1] // MXU_TILE):
            g = jnp.minimum(gu[:, k * MXU_TILE:k * MXU_TILE + half], SWIGLU_LIMIT)
            lin = jnp.clip(gu[:, k * MXU_TILE + half:(k + 1) * MXU_TILE], -SWIGLU_LIMIT, SWIGLU_LIMIT)
            acts.append((g * jax.nn.sigmoid(SWIGLU_ALPHA * g) * (lin + 1.0)).astype(BF16))
        o_ref[...] = _dot(jnp.concatenate(acts, axis=-1), wd_ref[0]) + bd_ref[0]

    @pl.when(i >= nu_ref[0])
    def _():
        o_ref[...] = jnp.zeros_like(o_ref)


def _experts(xb, block_exp, n_used, wgu, bgu, wd, bd):
    n_slots = xb.shape[0]
    bm = MOE_BLOCK
    wmap = lambda i, be, nu: (be[i], 0, 0)
    return pl.pallas_call(
        _expert_kernel,
        grid_spec=pltpu.PrefetchScalarGridSpec(
            num_scalar_prefetch=2, grid=(n_slots // bm,),
            in_specs=[pl.BlockSpec((bm, D_MODEL), lambda i, be, nu: (i, 0)),
                      pl.BlockSpec((1, D_MODEL, 2 * D_FF), wmap), pl.BlockSpec((1, 1, 2 * D_FF), wmap),
                      pl.BlockSpec((1, D_FF, D_MODEL), wmap), pl.BlockSpec((1, 1, D_MODEL), wmap)],
            out_specs=pl.BlockSpec((bm, D_MODEL), lambda i, be, nu: (i, 0))),
        out_shape=jax.ShapeDtypeStruct((n_slots, D_MODEL), F32),
        compiler_params=_cparams("arbitrary"), name="moe_experts",
    )(block_exp, n_used, xb, wgu, bgu, wd, bd)


def _combine_kernel(dest_ref, ys_ref, info_ref, x1_ref, mod_ref, gpost_ref, o_ref, idx_ref, buf_ref, isem, sem):
    i = pl.program_id(1)
    nb = pl.num_programs(1)
    rows = x1_ref.shape[1]
    cp = pltpu.make_async_copy(dest_ref.at[pl.program_id(0) * nb + i], idx_ref, isem)
    cp.start()
    cp.wait()

    def row_copy(r, kk, slot):
        return pltpu.make_async_copy(ys_ref.at[pl.ds(slot, 1)], buf_ref.at[kk, pl.ds(r, 1)], sem)

    def issue(r, carry):
        for kk in range(TOP_K):
            row_copy(r, kk, idx_ref[r * TOP_K + kk]).start()
        return carry
    lax.fori_loop(0, rows, issue, 0, unroll=DMA_UNROLL)

    def drain(r, carry):
        for kk in range(TOP_K):
            row_copy(r, kk, 0).wait()
        return carry
    lax.fori_loop(0, rows, drain, 0, unroll=DMA_UNROLL)

    info = info_ref[...]
    f = info[:, 0:1] * buf_ref[0]
    for kk in range(1, TOP_K):
        f = f + info[:, kk:kk + 1] * buf_ref[kk]
    g2 = mod_ref[0, 5:6, :]
    o_ref[0] = x1_ref[0] + g2 * _rms(f, gpost_ref[...])


def _combine(ys, dest, info, x1, mod, g_post):
    b, n, _ = x1.shape
    rows = min(DMA_ROWS, n)
    nb = n // rows
    mod_map = (lambda bi, i: (bi, 0, 0)) if mod.shape[0] > 1 else (lambda bi, i: (0, 0, 0))
    return pl.pallas_call(
        _combine_kernel, grid=(b, nb),
        in_specs=[pl.BlockSpec(memory_space=pl.ANY),
                  pl.BlockSpec(memory_space=pl.ANY),
                  pl.BlockSpec((rows, LANES), lambda bi, i: (bi * nb + i, 0)),
                  pl.BlockSpec((1, rows, D_MODEL), lambda bi, i: (bi, i, 0)),
                  pl.BlockSpec((1, 6, D_MODEL), mod_map),
                  pl.BlockSpec((1, D_MODEL), lambda bi, i: (0, 0))],
        out_specs=pl.BlockSpec((1, rows, D_MODEL), lambda bi, i: (bi, i, 0)),
        out_shape=jax.ShapeDtypeStruct((b, n, D_MODEL), F32),
        scratch_shapes=[pltpu.SMEM((rows * TOP_K,), jnp.int32),
                        pltpu.VMEM((TOP_K, rows, D_MODEL), F32),
                        pltpu.SemaphoreType.DMA(()), pltpu.SemaphoreType.DMA(())],
        compiler_params=_cparams("arbitrary", "arbitrary"), name="moe_combine",
    )(dest.reshape(b * nb, rows * TOP_K), ys, info, x1, mod, g_post)


def _moe_and_residual(h2, logits, x1, mod, g_post, ew):
    b, n, _ = x1.shape
    t = b * n
    info, cnt = _route(logits.reshape(t, LANES))
    counts = cnt[0, :N_EXPERTS].astype(jnp.int32)
    bm = MOE_BLOCK
    padded = (counts + bm - 1) // bm * bm
    pad_end = jnp.cumsum(padded)
    pad_start = pad_end - padded
    e_idx = info[:, TOP_K:2 * TOP_K].astype(jnp.int32)
    rank = info[:, 2 * TOP_K:3 * TOP_K].astype(jnp.int32)
    ids = jnp.arange(N_EXPERTS, dtype=jnp.int32)
    start_of = jnp.sum(jnp.where(e_idx[:, :, None] == ids, pad_start, 0), axis=-1)
    dest = (start_of + rank).reshape(-1)
    n_blocks = t * TOP_K // bm + N_EXPERTS
    n_slots = n_blocks * bm
    first_row = jnp.arange(n_blocks, dtype=jnp.int32) * bm
    block_exp = jnp.minimum(jnp.sum((pad_end[None, :] <= first_row[:, None]).astype(jnp.int32), axis=-1),
                            N_EXPERTS - 1)
    n_used = (pad_end[-1:] // bm).astype(jnp.int32)
    xb = _dispatch(h2.reshape(t, D_MODEL), dest, n_slots)
    layer_row, wgu, bgu, wd, bd = ew
    ys = _experts(xb, block_exp + layer_row, n_used, wgu, bgu, wd, bd)
    return _combine(ys, dest, info, x1, mod, g_post)


def _cos_sin(num, den, shape_like=None):
    ang = (2.0 * math.pi / den) * num.astype(F32)
    return jnp.cos(ang), jnp.sin(ang)


def _dense_dft_tables(n):
    nn = 2 * n
    f = jnp.arange(nn, dtype=jnp.int32)[:, None]
    s = jnp.arange(nn, dtype=jnp.int32)[None, :]
    c, sn = _cos_sin((f * s) % nn, nn)
    md = jnp.concatenate([c, -sn], axis=0).astype(BF16)
    ci, si = c[:n, :], sn[:n, :]
    mdi = jnp.concatenate([ci, -si], axis=1).astype(BF16)
    return {"md": md, "md_half": md[:, :n], "mdi": mdi}


def _fft_tables(n):
    nn = 2 * n
    n1, n2 = FFT_N1, FFT_N2
    assert nn == n1 * n2
    s2 = jnp.arange(n2, dtype=jnp.int32)[:, None, None]
    f1 = jnp.arange(n1, dtype=jnp.int32)[None, :, None]
    s1 = jnp.arange(n1, dtype=jnp.int32)[None, None, :]
    c, sn = _cos_sin((f1 * (n2 * s1 + s2)) % nn, nn)
    m1 = jnp.concatenate([c, -sn], axis=1).astype(BF16)
    a = jnp.arange(n2, dtype=jnp.int32)
    c2, sn2 = _cos_sin((a[:, None] * a[None, :]) % n2, n2)
    mf = jnp.concatenate([jnp.concatenate([c2, sn2], axis=1),
                          jnp.concatenate([-sn2, c2], axis=1)], axis=0).astype(BF16)
    mi = jnp.concatenate([jnp.concatenate([c2, -sn2], axis=1),
                          jnp.concatenate([sn2, c2], axis=1)], axis=0).astype(BF16)
    t2 = jnp.arange(n2, dtype=jnp.int32)[:, None, None]
    t1 = jnp.arange(n1 // 2, dtype=jnp.int32)[None, :, None]
    g1 = jnp.arange(n1, dtype=jnp.int32)[None, None, :]
    c3, sn3 = _cos_sin((g1 * (n2 * t1 + t2)) % nn, nn)
    m3 = jnp.concatenate([c3, -sn3], axis=2).astype(BF16)
    return {"m1_full": m1, "m1_half": m1[:, :, :n1 // 2], "mf": mf, "mi": mi, "m3": m3}


def _fnet_tables(n):
    j = jnp.arange(FN_WIDTH, dtype=jnp.int32)
    same = (j[:, None] // FN_GROUP_DIM) == (j[None, :] // FN_GROUP_DIM)
    c, s = _cos_sin(((j[:, None] % FN_GROUP_DIM) * (j[None, :] % FN_GROUP_DIM)) % FN_GROUP_DIM, FN_GROUP_DIM)
    bc = jnp.where(same, c, 0.0).astype(BF16)
    bs = jnp.where(same, s, 0.0).astype(BF16)
    r = FN_RADIX
    if n == r * r:
        s2 = jnp.arange(r, dtype=jnp.int32)[:, None, None]
        f1 = jnp.arange(r, dtype=jnp.int32)[None, :, None]
        s1 = jnp.arange(r, dtype=jnp.int32)[None, None, :]
        c1, sn1 = _cos_sin((f1 * (r * s1 + s2)) % n, n)
        m1 = jnp.concatenate([jnp.concatenate([c1, sn1], axis=2),
                              jnp.concatenate([-sn1, c1], axis=2)], axis=1).astype(BF16)
        a = jnp.arange(r, dtype=jnp.int32)
        c2, sn2 = _cos_sin((a[:, None] * a[None, :]) % r, r)
        m2 = jnp.concatenate([c2, sn2], axis=1).astype(BF16)
        return {"bc": bc, "bs": bs, "fn_m1": m1, "fn_m2": m2}
    p = jnp.arange(n, dtype=jnp.int32)
    cl, sl = _cos_sin((p[:, None] * p[None, :]) % n, n)
    return {"bc": bc, "bs": bs, "fn_lhs": jnp.concatenate([cl, -sl], axis=1).astype(BF16)}


def _filter_feats(n):
    r = jnp.arange(2 * n, dtype=jnp.int32)
    pos = jnp.where(r < n, r, 2 * n - r).astype(F32)
    t = pos / n
    bands = jnp.linspace(1e-4, FILT_BANDS - 1, FILT_BANDS, dtype=F32)
    ang = (2.0 * math.pi / n) * pos[:, None] * bands[None, :]
    feats = jnp.concatenate([t[:, None], jnp.cos(ang), -jnp.sin(ang)], axis=-1)
    return jnp.pad(feats, ((0, 0), (0, LANES - FILT_EMB)))


def _rope_tables(n, use_rope):
    if use_rope:
        tpos = jnp.arange(n, dtype=jnp.int32)
        rows = (tpos // GRID_W).astype(F32)
        cols = (tpos % GRID_W).astype(F32)
        inv = ROPE_BASE ** (-jnp.arange(AXIS_PAIRS, dtype=F32) / AXIS_PAIRS)
        ang = jnp.stack([rows[:, None] * inv, cols[:, None] * inv], axis=1)
        cos = jnp.broadcast_to(jnp.cos(ang)[:, :, None, :], (n, 2, 2, AXIS_PAIRS)).reshape(n, ROPE_DIM)
        sin = jnp.broadcast_to(jnp.sin(ang)[:, :, None, :], (n, 2, 2, AXIS_PAIRS)).reshape(n, ROPE_DIM)
    else:
        cos = jnp.ones((n, ROPE_DIM), F32)
        sin = jnp.zeros((n, ROPE_DIM), F32)
    pad = HEAD_PAD - QK_NOPE_DIM - ROPE_DIM
    cos_h = jnp.concatenate([jnp.ones((n, QK_NOPE_DIM), F32), cos, jnp.zeros((n, pad), F32)], axis=1)
    sin_h = jnp.concatenate([jnp.zeros((n, QK_NOPE_DIM), F32), sin, jnp.zeros((n, pad), F32)], axis=1)
    qscale = MLA_SCALE * math.log2(math.e)
    cosf = jnp.tile(cos_h, (1, N_HEADS)) * qscale
    sinf = jnp.tile(sin_h, (1, N_HEADS)) * qscale
    cosk = jnp.concatenate([cos, sin, jnp.zeros((n, LANES - 2 * ROPE_DIM), F32)], axis=1)
    return cosf, sinf, cosk


def _rot_matrix():
    r = np.zeros((ROPE_DIM, ROPE_DIM), np.float32)
    for a in range(2):
        for p in range(AXIS_PAIRS):
            lo = a * 2 * AXIS_PAIRS + p
            hi = lo + AXIS_PAIRS
            r[hi, lo] = -1.0
            r[lo, hi] = 1.0
    return jnp.asarray(r)


def _prep_layer(l, w):
    rot = _rot_matrix()
    w_in = w["w_in"][l]
    o4 = 3 * HY_WIDTH + FN_WIDTH + Q_LORA_RANK + KV_LORA_RANK
    w_kr = w_in[:, o4:]
    w_all = jnp.concatenate(
        [w_in[:, :o4], w_kr, w_kr @ rot, jnp.zeros((D_MODEL, LANES - 2 * ROPE_DIM), F32)], axis=1).astype(BF16)
    hd = QK_NOPE_DIM + ROPE_DIM
    pad = HEAD_PAD - hd
    wq = w["w_uq"][l].reshape(Q_LORA_RANK, N_HEADS, hd)
    zq = jnp.zeros((Q_LORA_RANK, N_HEADS, pad), F32)
    plain = jnp.concatenate([wq, zq], axis=2)
    rotated = jnp.concatenate([jnp.zeros((Q_LORA_RANK, N_HEADS, QK_NOPE_DIM), F32),
                               jnp.einsum("rhd,de->rhe", wq[:, :, QK_NOPE_DIM:], rot), zq], axis=2)
    hw = N_HEADS * HEAD_PAD
    w_uq_aug = jnp.concatenate([plain.reshape(Q_LORA_RANK, hw), rotated.reshape(Q_LORA_RANK, hw)], axis=1).astype(BF16)
    wkv = w["w_ukv"][l].reshape(KV_LORA_RANK, N_HEADS, QK_NOPE_DIM + V_HEAD_DIM)
    wk = jnp.concatenate([wkv[:, :, :QK_NOPE_DIM],
                          jnp.zeros((KV_LORA_RANK, N_HEADS, HEAD_PAD - QK_NOPE_DIM), F32)], axis=2).reshape(KV_LORA_RANK, hw)
    sel = np.zeros((LANES, N_HEADS, HEAD_PAD), np.float32)
    for j in range(ROPE_DIM):
        sel[j, :, QK_NOPE_DIM + j] = 1.0
        sel[ROPE_DIM + j, :, QK_NOPE_DIM + j] = 1.0
    wk_aug = jnp.concatenate([wk, jnp.asarray(sel).reshape(LANES, hw)], axis=0).astype(BF16)
    wv = jnp.concatenate([wkv[:, :, QK_NOPE_DIM:],
                          jnp.zeros((KV_LORA_RANK, N_HEADS, HEAD_PAD - V_HEAD_DIM), F32)], axis=2)
    wv = wv.reshape(KV_LORA_RANK, hw).astype(BF16)
    padh = LANES - FILT_HID
    w1 = jnp.pad(w["filt_w1"][l], ((0, LANES - FILT_EMB), (0, padh)))
    b1 = jnp.pad(w["filt_b1"][l], (0, padh)).reshape(1, LANES)
    w2 = jnp.pad(w["filt_w2"][l], ((0, padh), (0, padh)))
    b2 = jnp.pad(w["filt_b2"][l], (0, padh)).reshape(1, LANES)
    nblk = HY_ORDER * 2 * HY_WIDTH // LANES
    w3 = jnp.pad(w["filt_w3"][l], ((0, padh), (0, 0))).reshape(LANES, nblk, LANES).transpose(1, 0, 2)
    freq = jnp.pad(w["filt_freq"][l], ((0, 0), (0, padh)))
    nch = HY_WIDTH // LANES
    ld = w["hy_log_decay"][l].reshape(HY_ORDER, 2, nch, LANES).transpose(0, 2, 1, 3).reshape(HY_ORDER * nch, 2, LANES)
    rw = jnp.pad(w["router_w"][l], ((0, 0), (0, LANES - N_EXPERTS)))
    rb = jnp.pad(w["router_b"][l], (0, LANES - N_EXPERTS), constant_values=-1e30).reshape(1, LANES)
    ew = (l * N_EXPERTS, w["moe_w_gu_prepped"], w["moe_b_gu_prepped"], w["moe_w_down_bf16"], w["moe_b_down_rows"])
    return {
        "w_all": w_all, "w_uq_aug": w_uq_aug, "wk_aug": wk_aug, "wv": wv,
        "g_pre": w["g_pre_mix"][l].reshape(1, D_MODEL), "g_post": w["g_post_mix"][l].reshape(1, D_MODEL),
        "g_ffn": w["g_pre_ffn"][l].reshape(1, D_MODEL), "g_post_ffn": w["g_post_ffn"][l].reshape(1, D_MODEL),
        "g_q": w["q_norm_g"][l].reshape(1, Q_LORA_RANK), "g_kv": w["kv_norm_g"][l].reshape(1, KV_LORA_RANK),
        "conv_w": w["conv_w"][l], "conv_b": w["conv_b"][l], "skip": w["hy_skip"][l],
        "filt": (w1, b1, w2, b2, w3, freq, ld),
        "w_out": w["w_out"][l].astype(BF16), "rw": rw, "rb": rb, "ew": ew,
    }


def _trunk_layer(x, mod, p, pos, ctx):
    b, n, _ = x.shape
    u_hy, u_fn, q, ckv, ks = _in_proj(x, mod, p["g_pre"], p["w_all"], p["g_q"], p["w_uq_aug"], p["g_kv"],
                                      pos["cosf"], pos["sinf"])
    spec = _hyena_spectrum(n, pos["feats"], *p["filt"], pos["hy"])
    hy = _hyena_conv(u_hy, p["conv_w"], p["conv_b"], p["skip"], spec, pos["hy"])
    fn = _fnet(u_fn, pos["fn"])
    if ctx is None:
        ckv_all, ks_all, cosk = ckv, ks, pos["cosk"]
    else:
        c_ckv, c_kr = ctx
        past = c_ckv.shape[1]
        ckv_all = jnp.concatenate([c_ckv, ckv], axis=1)
        ks_all = jnp.concatenate([jnp.pad(c_kr, ((0, 0), (0, 0), (0, LANES - ROPE_DIM))), ks], axis=1)
        ident = jnp.concatenate([jnp.ones((past, ROPE_DIM), F32), jnp.zeros((past, LANES - ROPE_DIM), F32)], axis=1)
        cosk = jnp.concatenate([ident, pos["cosk"]], axis=0)
    k, v = _kv_up(ckv_all, ks_all, cosk, p["wk_aug"], p["wv"])
    att = _attention(q, k, v)
    x1, h2, logits = _out_proj(hy, fn, att, x, mod, p["w_out"], p["g_post"], p["g_ffn"], p["rw"], p["rb"])
    x2 = _moe_and_residual(h2, logits, x1, mod, p["g_post_ffn"], p["ew"])
    return x2, ckv, ks[..., :ROPE_DIM]


def _position_tables(n, use_rope, dense):
    cosf, sinf, cosk = _rope_tables(n, use_rope)
    return {"cosf": cosf, "sinf": sinf, "cosk": cosk, "feats": _filter_feats(n),
            "hy": _dense_dft_tables(n) if dense else _fft_tables(n), "fn": _fnet_tables(n)}


def kernel(x_prompt, x_sample, cache_ckv, cache_krope, c, c_ctx, w_mod, b_mod, g_pre_mix, g_post_mix, g_pre_ffn, g_post_ffn, w_in, conv_w, conv_b, filt_w1, filt_b1, filt_w2, filt_b2, filt_w3, filt_freq, hy_log_decay, hy_skip, q_norm_g, w_uq, kv_norm_g, w_ukv, w_out, router_w, router_b, moe_w_gu, moe_b_gu, moe_w_down, moe_b_down):
    w = {"g_pre_mix": g_pre_mix, "g_post_mix": g_post_mix, "g_pre_ffn": g_pre_ffn, "g_post_ffn": g_post_ffn,
         "w_in": w_in, "conv_w": conv_w, "conv_b": conv_b, "filt_w1": filt_w1, "filt_b1": filt_b1,
         "filt_w2": filt_w2, "filt_b2": filt_b2, "filt_w3": filt_w3, "filt_freq": filt_freq,
         "hy_log_decay": hy_log_decay, "hy_skip": hy_skip, "q_norm_g": q_norm_g, "w_uq": w_uq,
         "kv_norm_g": kv_norm_g, "w_ukv": w_ukv, "w_out": w_out, "router_w": router_w, "router_b": router_b,
         "moe_w_gu": moe_w_gu, "moe_b_gu": moe_b_gu, "moe_w_down": moe_w_down, "moe_b_down": moe_b_down}
    ne = DEPTH * N_EXPERTS
    half = MXU_TILE // 2
    w["moe_w_gu_prepped"] = _deinterleave_gate_up(moe_w_gu.reshape(ne, D_MODEL, 2 * D_FF))
    bgu = moe_b_gu.reshape(ne, 2 * D_FF // MXU_TILE, half, 2)
    w["moe_b_gu_prepped"] = bgu.transpose(0, 1, 3, 2).reshape(ne, 1, 2 * D_FF)
    w["moe_w_down_bf16"] = moe_w_down.reshape(ne, D_FF, D_MODEL).astype(BF16)
    w["moe_b_down_rows"] = moe_b_down.reshape(ne, 1, D_MODEL)
    nb = c.shape[0]
    rows = 16
    cc = jnp.concatenate([c_ctx[None, :], c, jnp.zeros((rows - 1 - nb, D_MODEL), F32)], axis=0)
    mod_all = _modulation(cc, w_mod, b_mod)
    layers = [_prep_layer(l, w) for l in range(DEPTH)]
    pos_p = _position_tables(x_prompt.shape[1], False, True)
    pos_s = _position_tables(x_sample.shape[1], True, False)

    y_prompt = x_prompt
    ckvs, krs = [], []
    for l in range(DEPTH):
        mod = mod_all[l, 0:1].reshape(1, 6, D_MODEL)
        y_prompt, ckv_l, kr_l = _trunk_layer(y_prompt, mod, layers[l], pos_p, None)
        ckvs.append(ckv_l)
        krs.append(kr_l)

    y_sample = x_sample
    for l in range(DEPTH):
        mod = mod_all[l, 1:1 + nb].reshape(nb, 6, D_MODEL)
        y_sample, _, _ = _trunk_layer(y_sample, mod, layers[l], pos_s, (cache_ckv[:, l], cache_krope[:, l]))

    return (y_prompt, y_sample, jnp.stack(ckvs, axis=1), jnp.stack(krs, axis=1))
```
